```python
import math
import jax, jax.numpy as jnp
from jax import lax
import numpy as np

D_MODEL = 1024
BATCH = 2
SEQ = 8192
DEPTH = 2

CHUNK = 64
Q_BLOCK = 128
ROPE_THETA = 500000.0
ROPE_FRACTION = 4
NORM_EPS = 1e-6
N_MIXERS = 2
DIFF_HEAD_DIM = 64
DIFF_HEADS = D_MODEL // (2 * DIFF_HEAD_DIM)
DIFF_V_DIM = 2 * DIFF_HEAD_DIM
DSA_HEAD_DIM = 64
DSA_HEADS = D_MODEL // DSA_HEAD_DIM
DSA_Q_RANK = D_MODEL // 4
IDX_HEADS = 8
IDX_DIM = 64
TOPK_MAX = 256
D_FF = 4 * D_MODEL

N_DIFF_LAYERS = (DEPTH + 1) // 2
N_DSA_LAYERS = DEPTH // 2

kernel_name = "hybrid_diffattn_dsa_chunk_causal"


def rms_norm(x, g=None):
    xf = x.astype(jnp.float32)
    y = xf * lax.rsqrt(jnp.mean(xf * xf, axis=-1, keepdims=True) + NORM_EPS)
    if g is not None:
        y = y * g.astype(jnp.float32)
    return y.astype(x.dtype)


def rope_tables(positions, head_dim):
    rot = head_dim // ROPE_FRACTION
    inv = ROPE_THETA ** (-jnp.arange(0, rot, 2, dtype=jnp.float32) / rot)
    ang = positions.astype(jnp.float32)[..., None] * inv
    return jnp.cos(ang)[:, :, None, :], jnp.sin(ang)[:, :, None, :]


def apply_partial_rope(x, cos, sin):
    half = cos.shape[-1]
    rot = 2 * half
    x1 = x[..., :half].astype(jnp.float32)
    x2 = x[..., half:rot].astype(jnp.float32)
    r = jnp.concatenate([x1 * cos - x2 * sin, x2 * cos + x1 * sin], axis=-1).astype(x.dtype)
    return jnp.concatenate([r, x[..., rot:]], axis=-1)


def diff_attention(h, w_in, q_g, k_g, lam_q1, lam_k1, lam_q2, lam_k2, subln_g, w_out,
                   cos, sin, lam_init):
    B, S, _ = h.shape
    H, d = DIFF_HEADS, DIFF_HEAD_DIM
    proj = h @ w_in
    q, k, v = jnp.split(proj, [2 * H * d, 4 * H * d], axis=-1)
    q = apply_partial_rope(rms_norm(q.reshape(B, S, 2 * H, d), q_g), cos, sin) * (d ** -0.5)
    k = apply_partial_rope(rms_norm(k.reshape(B, S, 2 * H, d), k_g), cos, sin)
    q = q.reshape(B, S, H, 2, d)
    k = k.reshape(B, S, H, 2, d)
    v = v.reshape(B, S, H, DIFF_V_DIM)
    lam = (jnp.exp(jnp.sum(lam_q1.astype(jnp.float32) * lam_k1.astype(jnp.float32)))
           - jnp.exp(jnp.sum(lam_q2.astype(jnp.float32) * lam_k2.astype(jnp.float32)))
           + lam_init)
    n_blocks = S // Q_BLOCK
    key_chunk = jnp.arange(S) // CHUNK
    q_blocks = q.reshape(B, n_blocks, Q_BLOCK, H, 2, d).swapaxes(0, 1)

    def block(args):
        qb, start = args
        q_chunk = (start + jnp.arange(Q_BLOCK)) // CHUNK
        mask = key_chunk[None, :] <= q_chunk[:, None]
        s = jnp.einsum('bqhcd,bkhcd->bhcqk', qb, k).astype(jnp.float32)
        p = jax.nn.softmax(jnp.where(mask, s, -jnp.inf), axis=-1)
        a = p[:, :, 0] - lam * p[:, :, 1]
        return jnp.einsum('bhqk,bkhe->bqhe', a.astype(v.dtype), v)

    o = lax.map(block, (q_blocks, jnp.arange(n_blocks) * Q_BLOCK))
    o = o.swapaxes(0, 1).reshape(B, S, H, DIFF_V_DIM)
    o = rms_norm(o, subln_g) * (1.0 - lam_init)
    return o.reshape(B, S, H * DIFF_V_DIM) @ w_out


def dsa_attention(h, w_in, cq_g, w_uq, w_uq_idx, q_g, k_g, w_out, cos, sin):
    B, S, _ = h.shape
    H, d, R, HI, DI = DSA_HEADS, DSA_HEAD_DIM, DSA_Q_RANK, IDX_HEADS, IDX_DIM
    proj = h @ w_in
    c_q, k, v, k_idx, w_idx = jnp.split(
        proj, [R, R + H * d, R + 2 * H * d, R + 2 * H * d + DI], axis=-1)
    c_q = rms_norm(c_q, cq_g)
    q = (c_q @ w_uq).reshape(B, S, H, d)
    q_idx = apply_partial_rope((c_q @ w_uq_idx).reshape(B, S, HI, DI), cos, sin)
    k_idx = apply_partial_rope(rms_norm(k_idx)[:, :, None, :], cos, sin)[:, :, 0]
    w_idx = w_idx * ((HI ** -0.5) * (DI ** -0.5))
    q = apply_partial_rope(rms_norm(q, q_g), cos, sin) * (d ** -0.5)
    k = apply_partial_rope(rms_norm(k.reshape(B, S, H, d), k_g), cos, sin)
    v = v.reshape(B, S, H, d)
    topk = min(TOPK_MAX, S // 4)
    n_blocks = S // Q_BLOCK
    key_chunk = jnp.arange(S) // CHUNK
    qb_all = q.reshape(B, n_blocks, Q_BLOCK, H, d).swapaxes(0, 1)
    qib_all = q_idx.reshape(B, n_blocks, Q_BLOCK, HI, DI).swapaxes(0, 1)
    wb_all = w_idx.reshape(B, n_blocks, Q_BLOCK, HI).swapaxes(0, 1)
    gather = jax.vmap(lambda arr, ids: arr[ids])

    def block(args):
        qb, qib, wb, start = args
        q_chunk = (start + jnp.arange(Q_BLOCK)) // CHUNK
        mask = key_chunk[None, :] <= q_chunk[:, None]
        dots = jnp.einsum('bqhe,bke->bqhk', qib, k_idx).astype(jnp.float32)
        score = jnp.einsum('bqh,bqhk->bqk', wb.astype(jnp.float32), jax.nn.relu(dots))
        score = jnp.where(mask, score, -jnp.inf)
        vals, idx = lax.top_k(score, topk)
        valid = jnp.isfinite(vals)
        kg = gather(k, idx)
        vg = gather(v, idx)
        s = jnp.einsum('bqhd,bqkhd->bhqk', qb, kg).astype(jnp.float32)
        p = jax.nn.softmax(jnp.where(valid[:, None], s, -jnp.inf), axis=-1)
        return jnp.einsum('bhqk,bqkhd->bqhd', p.astype(vg.dtype), vg)

    o = lax.map(block, (qb_all, qib_all, wb_all, jnp.arange(n_blocks) * Q_BLOCK))
    o = o.swapaxes(0, 1).reshape(B, S, H * d)
    return o @ w_out


def sq_relu_mlp(h, w1, w2):
    u = jax.nn.relu(h @ w1)
    return (u * u) @ w2


def setup_inputs(seed: int = 0) -> dict:
    key = jax.random.key(seed)
    ks = iter(jax.random.split(key, 32))
    D = D_MODEL

    def nrm(shape, scale):
        return jax.random.normal(next(ks), shape, jnp.float32) * scale

    def gain(shape):
        return 1.0 + nrm(shape, 0.02)

    nA, nB = N_DIFF_LAYERS, N_DSA_LAYERS
    dsa_in = DSA_Q_RANK + 2 * DSA_HEADS * DSA_HEAD_DIM + IDX_DIM + IDX_HEADS
    x = nrm((BATCH, SEQ, D), 1.0)
    offset = jax.random.randint(next(ks), (BATCH, 1), 0, 4096, dtype=jnp.int32)
    positions = (offset + jnp.arange(SEQ, dtype=jnp.int32)[None, :]).astype(jnp.int32)
    return {
        "x": x,
        "positions": positions,
        "norm_mix": gain((DEPTH, D)),
        "norm_mlp": gain((DEPTH, D)),
        "mlp_w1": nrm((DEPTH, D, D_FF), D ** -0.5),
        "mlp_w2": nrm((DEPTH, D_FF, D), D_FF ** -0.5),
        "diff_w_in": nrm((nA, D, 6 * DIFF_HEADS * DIFF_HEAD_DIM), D ** -0.5),
        "diff_q_norm": gain((nA, DIFF_HEAD_DIM)),
        "diff_k_norm": gain((nA, DIFF_HEAD_DIM)),
        "diff_lam_q1": nrm((nA, DIFF_HEAD_DIM), 0.1),
        "diff_lam_k1": nrm((nA, DIFF_HEAD_DIM), 0.1),
        "diff_lam_q2": nrm((nA, DIFF_HEAD_DIM), 0.1),
        "diff_lam_k2": nrm((nA, DIFF_HEAD_DIM), 0.1),
        "diff_subln": gain((nA, DIFF_V_DIM)),
        "diff_w_out": nrm((nA, DIFF_HEADS * DIFF_V_DIM, D), (DIFF_HEADS * DIFF_V_DIM) ** -0.5),
        "dsa_w_in": nrm((nB, D, dsa_in), D ** -0.5),
        "dsa_cq_norm": gain((nB, DSA_Q_RANK)),
        "dsa_w_uq": nrm((nB, DSA_Q_RANK, DSA_HEADS * DSA_HEAD_DIM), DSA_Q_RANK ** -0.5),
        "dsa_w_uq_idx": nrm((nB, DSA_Q_RANK, IDX_HEADS * IDX_DIM), DSA_Q_RANK ** -0.5),
        "dsa_q_norm": gain((nB, DSA_HEAD_DIM)),
        "dsa_k_norm": gain((nB, DSA_HEAD_DIM)),
        "dsa_w_out": nrm((nB, DSA_HEADS * DSA_HEAD_DIM, D), (DSA_HEADS * DSA_HEAD_DIM) ** -0.5),
    }


def reference(x, positions, norm_mix, norm_mlp, mlp_w1, mlp_w2,
              diff_w_in, diff_q_norm, diff_k_norm, diff_lam_q1, diff_lam_k1,
              diff_lam_q2, diff_lam_k2, diff_subln, diff_w_out,
              dsa_w_in, dsa_cq_norm, dsa_w_uq, dsa_w_uq_idx, dsa_q_norm, dsa_k_norm,
              dsa_w_out):
    cos, sin = rope_tables(positions, DIFF_HEAD_DIM)
    for i in range(DEPTH):
        h = rms_norm(x, norm_mix[i])
        j = i // N_MIXERS
        if i % N_MIXERS == 0:
            lam_init = 0.8 - 0.6 * math.exp(-0.3 * i)
            mix = diff_attention(h, diff_w_in[j], diff_q_norm[j], diff_k_norm[j],
                                 diff_lam_q1[j], diff_lam_k1[j], diff_lam_q2[j],
                                 diff_lam_k2[j], diff_subln[j], diff_w_out[j],
                                 cos, sin, lam_init)
        else:
            mix = dsa_attention(h, dsa_w_in[j], dsa_cq_norm[j], dsa_w_uq[j],
                                dsa_w_uq_idx[j], dsa_q_norm[j], dsa_k_norm[j],
                                dsa_w_out[j], cos, sin)
        x = x + mix
        x = x + sq_relu_mlp(rms_norm(x, norm_mlp[i]), mlp_w1[i], mlp_w2[i])
    return x
```

```python
import functools
import math

import jax
import jax.numpy as jnp
from jax import lax
from jax.experimental import pallas as pl
from jax.experimental.pallas import tpu as pltpu

F32 = jnp.float32
BF16 = jnp.bfloat16

D_MODEL = 1024
CHUNK = 64
HEAD_DIM = 64
ROPE_THETA = 500000.0
ROPE_ROT = HEAD_DIM // 4
ROPE_HALF = ROPE_ROT // 2
NORM_EPS = 1e-6
DIFF_HEADS = D_MODEL // (2 * HEAD_DIM)
DSA_HEADS = D_MODEL // HEAD_DIM
DSA_Q_RANK = D_MODEL // 4
IDX_HEADS = 8
IDX_DIM = 64
TOPK = 256
D_FF = 4 * D_MODEL
LANES = 128
PAIRS = D_MODEL // LANES
NEG = -1e30
VMEM_LIMIT = 56 * 1024 * 1024

KEY_NEG_INF = -(2 ** 31) + 0x7FFFFF


def _cparams(n_axes):
    return pltpu.CompilerParams(dimension_semantics=("arbitrary",) * n_axes,
                                vmem_limit_bytes=VMEM_LIMIT)


def _rms_rows(x):
    return x * lax.rsqrt(jnp.mean(x * x, axis=-1, keepdims=True) + NORM_EPS)


def _split_bf16(x):
    hi = x.astype(BF16)
    lo = (x - hi.astype(F32)).astype(BF16)
    return hi, lo


def _group_ones(n):
    r = lax.broadcasted_iota(jnp.int32, (n, n), 0) // HEAD_DIM
    c = lax.broadcasted_iota(jnp.int32, (n, n), 1) // HEAD_DIM
    return (r == c).astype(BF16)


def _head_rms(x, gmat):
    hi, lo = _split_bf16(x * x)
    ss = (jnp.dot(hi, gmat, preferred_element_type=F32)
          + jnp.dot(lo, gmat, preferred_element_type=F32))
    return x * lax.rsqrt(ss * (1.0 / HEAD_DIM) + NORM_EPS)


def _rope128(y, c, s1, s2):
    return y * c + pltpu.roll(y, ROPE_HALF, 1) * s1 + pltpu.roll(y, LANES - ROPE_HALF, 1) * s2


def _norm_rope_store(pc, gmat, gain, c, s1, s2, scale, out_ref, col0):
    y = pc if gmat is None else _head_rms(pc, gmat) * gain
    for hh in range(2):
        r = _rope128(y[:, hh * LANES:(hh + 1) * LANES], c, s1, s2)
        if scale != 1.0:
            r = r * scale
        out_ref[:, col0 + hh * LANES: col0 + (hh + 1) * LANES] = r.astype(out_ref.dtype)


def _proj_diff_kernel(x_ref, g_ref, w_ref, qg_ref, kg_ref, c_ref, s1_ref, s2_ref,
                      q_out, k_out, v_out):
    hb = (_rms_rows(x_ref[...]) * g_ref[...]).astype(BF16)
    gmat = _group_ones(256)
    c, s1, s2 = c_ref[...], s1_ref[...], s2_ref[...]
    qg, kg = qg_ref[...], kg_ref[...]
    for j in range(4):
        pc = jnp.dot(hb, w_ref[:, j * 256:(j + 1) * 256], preferred_element_type=F32)
        _norm_rope_store(pc, gmat, qg, c, s1, s2, HEAD_DIM ** -0.5, q_out, j * 256)
    for j in range(4):
        pc = jnp.dot(hb, w_ref[:, D_MODEL + j * 256: D_MODEL + (j + 1) * 256],
                     preferred_element_type=F32)
        _norm_rope_store(pc, gmat, kg, c, s1, s2, 1.0, k_out, j * 256)
    for j in range(4):
        pc = jnp.dot(hb, w_ref[:, 2 * D_MODEL + j * 256: 2 * D_MODEL + (j + 1) * 256],
                     preferred_element_type=F32)
        v_out[:, j * 256:(j + 1) * 256] = pc.astype(BF16)


def _proj_diff(x2d, g, w, qg, kg, c, s1, s2, tm=256):
    n = x2d.shape[0]
    row = lambda i: (i, 0)
    fixed = lambda i: (0, 0)
    return pl.pallas_call(
        _proj_diff_kernel,
        grid=(n // tm,),
        in_specs=[
            pl.BlockSpec((tm, D_MODEL), row),
            pl.BlockSpec((1, D_MODEL), fixed),
            pl.BlockSpec((D_MODEL, 3 * D_MODEL), fixed),
            pl.BlockSpec((1, 256), fixed),
            pl.BlockSpec((1, 256), fixed),
            pl.BlockSpec((tm, LANES), row),
            pl.BlockSpec((tm, LANES), row),
            pl.BlockSpec((tm, LANES), row),
        ],
        out_specs=[pl.BlockSpec((tm, D_MODEL), row)] * 3,
        out_shape=[jax.ShapeDtypeStruct((n, D_MODEL), BF16)] * 3,
        compiler_params=_cparams(1),
        name="proj_diff",
    )(x2d, g, w, qg, kg, c, s1, s2)


def _stack_pair(q):
    lane = lax.broadcasted_iota(jnp.int32, q.shape, 1)
    zero = jnp.zeros_like(q)
    return jnp.concatenate([jnp.where(lane < HEAD_DIM, q, zero),
                            jnp.where(lane >= HEAD_DIM, q, zero)], axis=0)


def _flash_update(s, v, m_ref, l_ref, acc_ref):
    m_prev = m_ref[...]
    m_new = jnp.maximum(m_prev, jnp.max(s, axis=-1, keepdims=True))
    alpha = jnp.exp(m_prev - m_new)
    p = jnp.exp(s - m_new)
    l_ref[...] = alpha * l_ref[...] + jnp.sum(p, axis=-1, keepdims=True)
    acc_ref[...] = alpha * acc_ref[...] + jnp.dot(p.astype(BF16), v, preferred_element_type=F32)
    m_ref[...] = m_new


def _qk(qq, k):
    return lax.dot_general(qq, k, (((1,), (1,)), ((), ())), preferred_element_type=F32)


def _diff_attn_kernel(q_ref, k_ref, v_ref, lq1_ref, lk1_ref, lq2_ref, lk2_ref, sg_ref, o_ref,
                      m_ref, l_ref, acc_ref, *, tq, lam_init):
    i = pl.program_id(2)
    qq = _stack_pair(q_ref[...])
    m_ref[...] = jnp.full(m_ref.shape, NEG, F32)
    l_ref[...] = jnp.zeros(l_ref.shape, F32)
    acc_ref[...] = jnp.zeros(acc_ref.shape, F32)

    def body(j, carry):
        start = pl.multiple_of(j * tq, tq)
        s = _qk(qq, k_ref[pl.ds(start, tq), :])
        _flash_update(s, v_ref[pl.ds(start, tq), :], m_ref, l_ref, acc_ref)
        return carry

    lax.fori_loop(0, i, body, 0)

    start = pl.multiple_of(i * tq, tq)
    s = _qk(qq, k_ref[pl.ds(start, tq), :])
    qc = (lax.broadcasted_iota(jnp.int32, s.shape, 0) % tq) // CHUNK
    kc = lax.broadcasted_iota(jnp.int32, s.shape, 1) // CHUNK
    s = jnp.where(kc <= qc, s, NEG)
    _flash_update(s, v_ref[pl.ds(start, tq), :], m_ref, l_ref, acc_ref)

    lam = (jnp.exp(jnp.sum(lq1_ref[...] * lk1_ref[...], axis=-1, keepdims=True))
           - jnp.exp(jnp.sum(lq2_ref[...] * lk2_ref[...], axis=-1, keepdims=True)) + lam_init)
    o = acc_ref[...] / l_ref[...]
    o = o[:tq] - lam * o[tq:]
    o = _rms_rows(o) * sg_ref[...] * (1.0 - lam_init)
    o_ref[...] = o.astype(o_ref.dtype)


def _diff_attention(q, k, v, lq1, lk1, lq2, lk2, sg, batch, seq, lam_init, tq=256):
    nq = seq // tq
    qmap = lambda b, h, i: (b * nq + i, h)
    kvmap = lambda b, h, i: (b, h)
    fixed = lambda b, h, i: (0, 0)
    kern = functools.partial(_diff_attn_kernel, tq=tq, lam_init=lam_init)
    return pl.pallas_call(
        kern,
        grid=(batch, DIFF_HEADS, nq),
        in_specs=[
            pl.BlockSpec((tq, LANES), qmap),
            pl.BlockSpec((seq, LANES), kvmap),
            pl.BlockSpec((seq, LANES), kvmap),
            pl.BlockSpec((1, HEAD_DIM), fixed),
            pl.BlockSpec((1, HEAD_DIM), fixed),
            pl.BlockSpec((1, HEAD_DIM), fixed),
            pl.BlockSpec((1, HEAD_DIM), fixed),
            pl.BlockSpec((1, LANES), fixed),
        ],
        out_specs=pl.BlockSpec((tq, LANES), qmap),
        out_shape=jax.ShapeDtypeStruct((batch * seq, D_MODEL), BF16),
        scratch_shapes=[
            pltpu.VMEM((2 * tq, 1), F32),
            pltpu.VMEM((2 * tq, 1), F32),
            pltpu.VMEM((2 * tq, LANES), F32),
        ],
        compiler_params=_cparams(3),
        name="diff_attention",
    )(q, k, v, lq1, lk1, lq2, lk2, sg)


def _out_mlp_kernel(x_ref, o_ref, wo_ref, g_ref, w1_ref, w2_ref, out_ref):
    x1 = x_ref[...] + jnp.dot(o_ref[...], wo_ref[...], preferred_element_type=F32)
    hn = (_rms_rows(x1) * g_ref[...]).astype(BF16)
    acc = x1
    for j in range(D_FF // D_MODEL):
        u = jnp.dot(hn, w1_ref[:, j * D_MODEL:(j + 1) * D_MODEL], preferred_element_type=F32)
        u = jnp.maximum(u, 0.0)
        acc = acc + jnp.dot((u * u).astype(BF16), w2_ref[j * D_MODEL:(j + 1) * D_MODEL, :],
                            preferred_element_type=F32)
    out_ref[...] = acc


def _out_mlp(x2d, o, wo, g, w1, w2, tm=512):
    n = x2d.shape[0]
    row = lambda i: (i, 0)
    fixed = lambda i: (0, 0)
    once = pl.Buffered(1)
    return pl.pallas_call(
        _out_mlp_kernel,
        grid=(n // tm,),
        in_specs=[
            pl.BlockSpec((tm, D_MODEL), row),
            pl.BlockSpec((tm, D_MODEL), row),
            pl.BlockSpec((D_MODEL, D_MODEL), fixed, pipeline_mode=once),
            pl.BlockSpec((1, D_MODEL), fixed),
            pl.BlockSpec((D_MODEL, D_FF), fixed, pipeline_mode=once),
            pl.BlockSpec((D_FF, D_MODEL), fixed, pipeline_mode=once),
        ],
        out_specs=pl.BlockSpec((tm, D_MODEL), row),
        out_shape=jax.ShapeDtypeStruct((n, D_MODEL), F32),
        compiler_params=_cparams(1),
        name="out_mlp",
    )(x2d, o, wo, g, w1, w2)


DSA_TAIL = D_MODEL // 4 + 2 * D_MODEL


def _proj_dsa_kernel(x_ref, g_ref, w_ref, cqg_ref, wuq_ref, wui_ref, qg_ref, kg_ref,
                     c_ref, s1_ref, s2_ref, q_out, qi_out, k_out, v_out, ki_out, wi_out):
    hb = (_rms_rows(x_ref[...]) * g_ref[...]).astype(BF16)
    gmat = _group_ones(256)
    c, s1, s2 = c_ref[...], s1_ref[...], s2_ref[...]

    cq = jnp.dot(hb, w_ref[:, 0:DSA_Q_RANK], preferred_element_type=F32)
    cqb = (_rms_rows(cq) * cqg_ref[...]).astype(BF16)
    qg, kg = qg_ref[...], kg_ref[...]
    for j in range(4):
        pc = jnp.dot(cqb, wuq_ref[:, j * 256:(j + 1) * 256], preferred_element_type=F32)
        _norm_rope_store(pc, gmat, qg, c, s1, s2, HEAD_DIM ** -0.5, q_out, j * 256)
    for j in range(2):
        pc = jnp.dot(cqb, wui_ref[:, j * 256:(j + 1) * 256], preferred_element_type=F32)
        _norm_rope_store(pc, None, None, c, s1, s2, 1.0, qi_out, j * 256)
    for j in range(4):
        col = DSA_Q_RANK + j * 256
        pc = jnp.dot(hb, w_ref[:, col:col + 256], preferred_element_type=F32)
        _norm_rope_store(pc, gmat, kg, c, s1, s2, 1.0, k_out, j * 256)
    for j in range(4):
        col = DSA_Q_RANK + D_MODEL + j * 256
        pc = jnp.dot(hb, w_ref[:, col:col + 256], preferred_element_type=F32)
        v_out[:, j * 256:(j + 1) * 256] = pc.astype(BF16)

    t = jnp.dot(hb, w_ref[:, DSA_TAIL:DSA_TAIL + LANES], preferred_element_type=F32)
    lane = lax.broadcasted_iota(jnp.int32, t.shape, 1)
    is_k = lane < IDX_DIM
    ss = jnp.sum(jnp.where(is_k, t * t, 0.0), axis=-1, keepdims=True)
    kn = t * lax.rsqrt(ss * (1.0 / IDX_DIM) + NORM_EPS)
    kr = _rope128(kn, c, s1, s2)
    ki_out[...] = jnp.where(is_k, kr, pltpu.roll(kr, IDX_DIM, 1)).astype(BF16)
    wi_out[...] = t * ((IDX_HEADS ** -0.5) * (IDX_DIM ** -0.5))


def _proj_dsa(x2d, g, w, cqg, wuq, wui, qg, kg, c, s1, s2, tm=256):
    n = x2d.shape[0]
    row = lambda i: (i, 0)
    fixed = lambda i: (0, 0)
    wcols = w.shape[1]
    idx_w = IDX_HEADS * IDX_DIM
    return pl.pallas_call(
        _proj_dsa_kernel,
        grid=(n // tm,),
        in_specs=[
            pl.BlockSpec((tm, D_MODEL), row),
            pl.BlockSpec((1, D_MODEL), fixed),
            pl.BlockSpec((D_MODEL, wcols), fixed),
            pl.BlockSpec((1, DSA_Q_RANK), fixed),
            pl.BlockSpec((DSA_Q_RANK, D_MODEL), fixed),
            pl.BlockSpec((DSA_Q_RANK, idx_w), fixed),
            pl.BlockSpec((1, 256), fixed),
            pl.BlockSpec((1, 256), fixed),
            pl.BlockSpec((tm, LANES), row),
            pl.BlockSpec((tm, LANES), row),
            pl.BlockSpec((tm, LANES), row),
        ],
        out_specs=[
            pl.BlockSpec((tm, D_MODEL), row),
            pl.BlockSpec((tm, idx_w), row),
            pl.BlockSpec((tm, D_MODEL), row),
            pl.BlockSpec((tm, D_MODEL), row),
            pl.BlockSpec((tm, LANES), row),
            pl.BlockSpec((tm, LANES), row),
        ],
        out_shape=[
            jax.ShapeDtypeStruct((n, D_MODEL), BF16),
            jax.ShapeDtypeStruct((n, idx_w), BF16),
            jax.ShapeDtypeStruct((n, D_MODEL), BF16),
            jax.ShapeDtypeStruct((n, D_MODEL), BF16),
            jax.ShapeDtypeStruct((n, LANES), BF16),
            jax.ShapeDtypeStruct((n, LANES), F32),
        ],
        compiler_params=_cparams(1),
        name="proj_dsa",
    )(x2d, g, w, cqg, wuq, wui, qg, kg, c, s1, s2)


def _sortable_key(x):
    bits = lax.bitcast_convert_type(x, jnp.int32)
    return bits ^ (lax.shift_right_arithmetic(bits, 31) & 0x7FFFFFFF)


def _dsa_kernel(q_ref, qi_ref, wi_ref, ki_ref, k_ref, v_ref, o_ref,
                key_ref, m_ref, l_ref, acc_ref, *, tq, tk):
    i = pl.program_id(1)
    ntiles = (i * tq + tq + tk - 1) // tk
    row0 = i * tq

    wi = wi_ref[...]
    qis, wst = [], []
    for c in range(IDX_HEADS // 2):
        qis.append(_stack_pair(qi_ref[:, c * LANES:(c + 1) * LANES]))
        w0 = wi[:, IDX_DIM + 2 * c: IDX_DIM + 2 * c + 1]
        w1 = wi[:, IDX_DIM + 2 * c + 1: IDX_DIM + 2 * c + 2]
        wst.append(jnp.concatenate([w0, w1], axis=0))

    def score_body(t, carry):
        start = pl.multiple_of(t * tk, tk)
        kd = ki_ref[pl.ds(start, tk), :]
        tot = jnp.zeros((2 * tq, tk), F32)
        for c in range(IDX_HEADS // 2):
            tot = tot + wst[c] * jnp.maximum(_qk(qis[c], kd), 0.0)
        score = tot[:tq] + tot[tq:]
        qc = (row0 + lax.broadcasted_iota(jnp.int32, score.shape, 0)) // CHUNK
        kc = (start + lax.broadcasted_iota(jnp.int32, score.shape, 1)) // CHUNK
        key_ref[:, pl.ds(start, tk)] = jnp.where(kc <= qc, _sortable_key(score), KEY_NEG_INF)
        return carry

    lax.fori_loop(0, ntiles, score_body, 0)

    def bit_body(b, thr):
        cand = thr + lax.shift_left(jnp.int32(1), 31 - b)

        def count_body(t, cnt):
            start = pl.multiple_of(t * tk, tk)
            ge = (key_ref[:, pl.ds(start, tk)] >= cand).astype(jnp.int32)
            for u in range(tk // LANES):
                cnt = cnt + ge[:, u * LANES:(u + 1) * LANES]
            return cnt

        cnt = lax.fori_loop(0, ntiles, count_body, jnp.zeros((tq, LANES), jnp.int32))
        total = jnp.sum(cnt, axis=-1, keepdims=True)
        return jnp.where(total >= TOPK, cand, thr)

    thr = lax.fori_loop(0, 32, bit_body, jnp.full((tq, 1), -(2 ** 31), jnp.int32))
    thr = jnp.maximum(thr, KEY_NEG_INF + 1)

    m_ref[...] = jnp.full(m_ref.shape, NEG, F32)
    l_ref[...] = jnp.zeros(l_ref.shape, F32)
    acc_ref[...] = jnp.zeros(acc_ref.shape, F32)
    qqs = [_stack_pair(q_ref[:, c * LANES:(c + 1) * LANES]) for c in range(PAIRS)]

    def attn_body(t, carry):
        start = pl.multiple_of(t * tk, tk)
        sel = key_ref[:, pl.ds(start, tk)] >= thr
        bias = jnp.where(sel, 0.0, NEG)
        bias = jnp.concatenate([bias, bias], axis=0)
        for c in range(PAIRS):
            s = _qk(qqs[c], k_ref[pl.ds(start, tk), c * LANES:(c + 1) * LANES]) + bias
            _flash_update(s, v_ref[pl.ds(start, tk), c * LANES:(c + 1) * LANES],
                          m_ref.at[c], l_ref.at[c], acc_ref.at[c])
        return carry

    lax.fori_loop(0, ntiles, attn_body, 0)

    lane = lax.broadcasted_iota(jnp.int32, (tq, LANES), 1)
    for c in range(PAIRS):
        o = acc_ref[c] / l_ref[c]
        o_ref[:, c * LANES:(c + 1) * LANES] = jnp.where(lane < HEAD_DIM, o[:tq], o[tq:]).astype(o_ref.dtype)


def _dsa_attention(q, qi, wi, ki, k, v, batch, seq, tq=128, tk=512):
    nq = seq // tq
    qmap = lambda b, i: (b * nq + i, 0)
    kvmap = lambda b, i: (b, 0)
    once = pl.Buffered(1)
    kern = functools.partial(_dsa_kernel, tq=tq, tk=tk)
    return pl.pallas_call(
        kern,
        grid=(batch, nq),
        in_specs=[
            pl.BlockSpec((tq, D_MODEL), qmap),
            pl.BlockSpec((tq, IDX_HEADS * IDX_DIM), qmap),
            pl.BlockSpec((tq, LANES), qmap),
            pl.BlockSpec((seq, LANES), kvmap),
            pl.BlockSpec((seq, D_MODEL), kvmap, pipeline_mode=once),
            pl.BlockSpec((seq, D_MODEL), kvmap, pipeline_mode=once),
        ],
        out_specs=pl.BlockSpec((tq, D_MODEL), qmap),
        out_shape=jax.ShapeDtypeStruct((batch * seq, D_MODEL), BF16),
        scratch_shapes=[
            pltpu.VMEM((tq, seq), jnp.int32),
            pltpu.VMEM((PAIRS, 2 * tq, 1), F32),
            pltpu.VMEM((PAIRS, 2 * tq, 1), F32),
            pltpu.VMEM((PAIRS, 2 * tq, LANES), F32),
        ],
        compiler_params=_cparams(2),
        name="dsa_attention",
    )(q, qi, wi, ki, k, v)


def _rope_lane_tables(positions):
    n = positions.size
    inv = ROPE_THETA ** (-jnp.arange(0, ROPE_ROT, 2, dtype=F32) / ROPE_ROT)
    ang = positions.reshape(n, 1).astype(F32) * inv
    cos, sin = jnp.cos(ang), jnp.sin(ang)
    rest = HEAD_DIM - ROPE_ROT
    c = jnp.concatenate([cos, cos, jnp.ones((n, rest), F32)], axis=1)
    s1 = jnp.concatenate([jnp.zeros((n, ROPE_HALF), F32), sin, jnp.zeros((n, rest), F32)], axis=1)
    s2 = jnp.concatenate([-sin, jnp.zeros((n, HEAD_DIM - ROPE_HALF), F32)], axis=1)
    return tuple(jnp.tile(t, (1, LANES // HEAD_DIM)) for t in (c, s1, s2))


def _tile_gain(g, width):
    return jnp.tile(g.astype(F32), width // g.shape[0]).reshape(1, width)


def kernel(x, positions, norm_mix, norm_mlp, mlp_w1, mlp_w2, diff_w_in, diff_q_norm, diff_k_norm, diff_lam_q1, diff_lam_k1, diff_lam_q2, diff_lam_k2, diff_subln, diff_w_out, dsa_w_in, dsa_cq_norm, dsa_w_uq, dsa_w_uq_idx, dsa_q_norm, dsa_k_norm, dsa_w_out):
    batch, seq, d = x.shape
    n = batch * seq
    x2d = x.reshape(n, d)
    c, s1, s2 = _rope_lane_tables(positions)
    row = lambda a: a.astype(F32).reshape(1, -1)

    lam_init = 0.8 - 0.6 * math.exp(-0.3 * 0)
    q, k, v = _proj_diff(x2d, row(norm_mix[0]), diff_w_in[0].astype(BF16),
                         _tile_gain(diff_q_norm[0], 256), _tile_gain(diff_k_norm[0], 256), c, s1, s2)
    o = _diff_attention(q, k, v, row(diff_lam_q1[0]), row(diff_lam_k1[0]), row(diff_lam_q2[0]),
                        row(diff_lam_k2[0]), row(diff_subln[0]), batch, seq, lam_init)
    x2d = _out_mlp(x2d, o, diff_w_out[0].astype(BF16), row(norm_mlp[0]),
                   mlp_w1[0].astype(BF16), mlp_w2[0].astype(BF16))

    w_in = dsa_w_in[0]
    pad = jnp.zeros((d, LANES - IDX_DIM - IDX_HEADS), w_in.dtype)
    w_cat = jnp.concatenate([w_in, pad], axis=1).astype(BF16)
    q, qi, k, v, ki, wi = _proj_dsa(x2d, row(norm_mix[1]), w_cat, row(dsa_cq_norm[0]),
                                    dsa_w_uq[0].astype(BF16), dsa_w_uq_idx[0].astype(BF16),
                                    _tile_gain(dsa_q_norm[0], 256), _tile_gain(dsa_k_norm[0], 256),
                                    c, s1, s2)
    o = _dsa_attention(q, qi, wi, ki, k, v, batch, seq)
    x2d = _out_mlp(x2d, o, dsa_w_out[0].astype(BF16), row(norm_mlp[1]),
                   mlp_w1[1].astype(BF16), mlp_w2[1].astype(BF16))
    return x2d.reshape(batch, seq, d)
```

```python
import functools
import math

import jax
import jax.numpy as jnp
from jax import lax
from jax.experimental import pallas as pl
from jax.experimental.pallas import tpu as pltpu

F32 = jnp.float32
BF16 = jnp.bfloat16

D_MODEL = 1024
CHUNK = 64
HEAD_DIM = 64
ROPE_THETA = 500000.0
ROPE_ROT = HEAD_DIM // 4
ROPE_HALF = ROPE_ROT // 2
NORM_EPS = 1e-6
DIFF_HEADS = D_MODEL // (2 * HEAD_DIM)
DSA_HEADS = D_MODEL // HEAD_DIM
DSA_Q_RANK = D_MODEL // 4
IDX_HEADS = 8
IDX_DIM = 64
TOPK = 256
D_FF = 4 * D_MODEL
LANES = 128
PAIRS = D_MODEL // LANES
NEG = -1e30
Q_SCALE = HEAD_DIM ** -0.5 * math.log2(math.e)
VMEM_LIMIT = 56 * 1024 * 1024

KEY_NEG_INF = -(2 ** 31) + 0x7FFFFF


def _cparams(n_axes):
    return pltpu.CompilerParams(dimension_semantics=("arbitrary",) * n_axes,
                                vmem_limit_bytes=VMEM_LIMIT)


def _rms_rows(x):
    return x * lax.rsqrt(jnp.mean(x * x, axis=-1, keepdims=True) + NORM_EPS)


def _split_bf16(x):
    hi = x.astype(BF16)
    lo = (x - hi.astype(F32)).astype(BF16)
    return hi, lo


def _group_ones(n):
    r = lax.broadcasted_iota(jnp.int32, (n, n), 0) // HEAD_DIM
    c = lax.broadcasted_iota(jnp.int32, (n, n), 1) // HEAD_DIM
    return (r == c).astype(BF16)


def _head_rms(x, gmat):
    hi, lo = _split_bf16(x * x)
    ss = (jnp.dot(hi, gmat, preferred_element_type=F32)
          + jnp.dot(lo, gmat, preferred_element_type=F32))
    return x * lax.rsqrt(ss * (1.0 / HEAD_DIM) + NORM_EPS)


def _rope128(y, c, s1, s2):
    return y * c + pltpu.roll(y, ROPE_HALF, 1) * s1 + pltpu.roll(y, LANES - ROPE_HALF, 1) * s2


def _norm_rope_store(pc, gmat, gain, c, s1, s2, scale, out_ref, col0):
    y = pc if gmat is None else _head_rms(pc, gmat) * gain
    for hh in range(2):
        r = _rope128(y[:, hh * LANES:(hh + 1) * LANES], c, s1, s2)
        if scale != 1.0:
            r = r * scale
        out_ref[:, col0 + hh * LANES: col0 + (hh + 1) * LANES] = r.astype(out_ref.dtype)


def _proj_diff_kernel(x_ref, g_ref, w_ref, qg_ref, kg_ref, c_ref, s1_ref, s2_ref,
                      q_out, k_out, v_out):
    hb = (_rms_rows(x_ref[...]) * g_ref[...]).astype(BF16)
    gmat = _group_ones(256)
    c, s1, s2 = c_ref[...], s1_ref[...], s2_ref[...]
    qg, kg = qg_ref[...], kg_ref[...]
    for j in range(4):
        pc = jnp.dot(hb, w_ref[:, j * 256:(j + 1) * 256], preferred_element_type=F32)
        _norm_rope_store(pc, gmat, qg, c, s1, s2, Q_SCALE, q_out, j * 256)
    for j in range(4):
        pc = jnp.dot(hb, w_ref[:, D_MODEL + j * 256: D_MODEL + (j + 1) * 256],
                     preferred_element_type=F32)
        _norm_rope_store(pc, gmat, kg, c, s1, s2, 1.0, k_out, j * 256)
    for j in range(4):
        pc = jnp.dot(hb, w_ref[:, 2 * D_MODEL + j * 256: 2 * D_MODEL + (j + 1) * 256],
                     preferred_element_type=F32)
        v_out[:, j * 256:(j + 1) * 256] = pc.astype(BF16)


def _proj_diff(x2d, g, w, qg, kg, c, s1, s2, tm=256):
    n = x2d.shape[0]
    row = lambda i: (i, 0)
    fixed = lambda i: (0, 0)
    return pl.pallas_call(
        _proj_diff_kernel,
        grid=(n // tm,),
        in_specs=[
            pl.BlockSpec((tm, D_MODEL), row),
            pl.BlockSpec((1, D_MODEL), fixed),
            pl.BlockSpec((D_MODEL, 3 * D_MODEL), fixed),
            pl.BlockSpec((1, 256), fixed),
            pl.BlockSpec((1, 256), fixed),
            pl.BlockSpec((tm, LANES), row),
            pl.BlockSpec((tm, LANES), row),
            pl.BlockSpec((tm, LANES), row),
        ],
        out_specs=[pl.BlockSpec((tm, D_MODEL), row)] * 3,
        out_shape=[jax.ShapeDtypeStruct((n, D_MODEL), BF16)] * 3,
        compiler_params=_cparams(1),
        name="proj_diff",
    )(x2d, g, w, qg, kg, c, s1, s2)


def _stack_pair(q):
    lane = lax.broadcasted_iota(jnp.int32, q.shape, 1)
    zero = jnp.zeros_like(q)
    return jnp.concatenate([jnp.where(lane < HEAD_DIM, q, zero),
                            jnp.where(lane >= HEAD_DIM, q, zero)], axis=0)


def _qk(qq, k):
    return lax.dot_general(qq, k, (((1,), (1,)), ((), ())), preferred_element_type=F32)


def _flash_init(q_ref, qq_ref, m_ref, l_ref, acc_ref):
    for c in range(PAIRS):
        qq_ref[c] = _stack_pair(q_ref[:, c * LANES:(c + 1) * LANES])
    m_ref[...] = jnp.full(m_ref.shape, NEG, F32)
    l_ref[...] = jnp.zeros(l_ref.shape, F32)
    acc_ref[...] = jnp.zeros(acc_ref.shape, F32)


def _flash_tile(qq_ref, k_ref, v_ref, start, tk, bias, m_ref, l_ref, acc_ref):
    for c in range(PAIRS):
        cols = slice(c * LANES, (c + 1) * LANES)
        s = _qk(qq_ref[c], k_ref[pl.ds(start, tk), cols])
        if bias is not None:
            s = s + bias
        chunks = [s[:, u * LANES:(u + 1) * LANES] for u in range(tk // LANES)]
        m_prev = m_ref[c]
        m_cur = jnp.max(functools.reduce(jnp.maximum, chunks), axis=-1, keepdims=True)
        m_new = jnp.maximum(m_prev, m_cur)
        alpha = jnp.exp2(m_prev - m_new)
        ps = [jnp.exp2(ch - m_new) for ch in chunks]
        l_ref[c] = alpha * l_ref[c] + functools.reduce(jnp.add, ps)
        p = jnp.concatenate(ps, axis=1).astype(BF16)
        acc_ref[c] = alpha * acc_ref[c] + jnp.dot(p, v_ref[pl.ds(start, tk), cols],
                                                   preferred_element_type=F32)
        m_ref[c] = m_new


def _flash_out(c, l_ref, acc_ref):
    return acc_ref[c] / jnp.sum(l_ref[c], axis=-1, keepdims=True)


def _flash_scratch(tq):
    return [
        pltpu.VMEM((PAIRS, 2 * tq, LANES), BF16),
        pltpu.VMEM((PAIRS, 2 * tq, LANES), F32),
        pltpu.VMEM((PAIRS, 2 * tq, LANES), F32),
        pltpu.VMEM((PAIRS, 2 * tq, LANES), F32),
    ]


def _diff_attn_kernel(q_ref, k_ref, v_ref, lq1_ref, lk1_ref, lq2_ref, lk2_ref, sg_ref, o_ref,
                      qq_ref, m_ref, l_ref, acc_ref, *, tq, tk, lam_init):
    i = pl.program_id(1)
    _flash_init(q_ref, qq_ref, m_ref, l_ref, acc_ref)
    n_full = (i * tq) // tk

    def body(t, carry):
        _flash_tile(qq_ref, k_ref, v_ref, pl.multiple_of(t * tk, tk), tk, None, m_ref, l_ref, acc_ref)
        return carry

    lax.fori_loop(0, n_full, body, 0)

    start = pl.multiple_of(n_full * tk, tk)
    qc = (i * tq + lax.broadcasted_iota(jnp.int32, (2 * tq, tk), 0) % tq) // CHUNK
    kc = (start + lax.broadcasted_iota(jnp.int32, (2 * tq, tk), 1)) // CHUNK
    bias = jnp.where(kc <= qc, 0.0, NEG)
    _flash_tile(qq_ref, k_ref, v_ref, start, tk, bias, m_ref, l_ref, acc_ref)

    lam = (jnp.exp(jnp.sum(lq1_ref[...] * lk1_ref[...], axis=-1, keepdims=True))
           - jnp.exp(jnp.sum(lq2_ref[...] * lk2_ref[...], axis=-1, keepdims=True)) + lam_init)
    for c in range(PAIRS):
        o = _flash_out(c, l_ref, acc_ref)
        o = o[:tq] - lam * o[tq:]
        o = _rms_rows(o) * sg_ref[...] * (1.0 - lam_init)
        o_ref[:, c * LANES:(c + 1) * LANES] = o.astype(o_ref.dtype)


def _diff_attention(q, k, v, lq1, lk1, lq2, lk2, sg, batch, seq, lam_init, tq=256, tk=512):
    nq = seq // tq
    qmap = lambda b, i: (b * nq + i, 0)
    kvmap = lambda b, i: (b, 0)
    fixed = lambda b, i: (0, 0)
    once = pl.Buffered(1)
    kern = functools.partial(_diff_attn_kernel, tq=tq, tk=tk, lam_init=lam_init)
    return pl.pallas_call(
        kern,
        grid=(batch, nq),
        in_specs=[
            pl.BlockSpec((tq, D_MODEL), qmap),
            pl.BlockSpec((seq, D_MODEL), kvmap, pipeline_mode=once),
            pl.BlockSpec((seq, D_MODEL), kvmap, pipeline_mode=once),
            pl.BlockSpec((1, HEAD_DIM), fixed),
            pl.BlockSpec((1, HEAD_DIM), fixed),
            pl.BlockSpec((1, HEAD_DIM), fixed),
            pl.BlockSpec((1, HEAD_DIM), fixed),
            pl.BlockSpec((1, LANES), fixed),
        ],
        out_specs=pl.BlockSpec((tq, D_MODEL), qmap),
        out_shape=jax.ShapeDtypeStruct((batch * seq, D_MODEL), BF16),
        scratch_shapes=_flash_scratch(tq),
        compiler_params=_cparams(2),
        name="diff_attention",
    )(q, k, v, lq1, lk1, lq2, lk2, sg)


def _out_mlp_kernel(x_ref, o_ref, wo_ref, g_ref, w1_ref, w2_ref, out_ref):
    x1 = x_ref[...] + jnp.dot(o_ref[...], wo_ref[...], preferred_element_type=F32)
    hn = (_rms_rows(x1) * g_ref[...]).astype(BF16)
    acc = x1
    for j in range(D_FF // D_MODEL):
        u = jnp.dot(hn, w1_ref[:, j * D_MODEL:(j + 1) * D_MODEL], preferred_element_type=F32)
        u = jnp.maximum(u, 0.0)
        acc = acc + jnp.dot((u * u).astype(BF16), w2_ref[j * D_MODEL:(j + 1) * D_MODEL, :],
                            preferred_element_type=F32)
    out_ref[...] = acc


def _out_mlp(x2d, o, wo, g, w1, w2, tm=512):
    n = x2d.shape[0]
    row = lambda i: (i, 0)
    fixed = lambda i: (0, 0)
    once = pl.Buffered(1)
    return pl.pallas_call(
        _out_mlp_kernel,
        grid=(n // tm,),
        in_specs=[
            pl.BlockSpec((tm, D_MODEL), row),
            pl.BlockSpec((tm, D_MODEL), row),
            pl.BlockSpec((D_MODEL, D_MODEL), fixed, pipeline_mode=once),
            pl.BlockSpec((1, D_MODEL), fixed),
            pl.BlockSpec((D_MODEL, D_FF), fixed, pipeline_mode=once),
            pl.BlockSpec((D_FF, D_MODEL), fixed, pipeline_mode=once),
        ],
        out_specs=pl.BlockSpec((tm, D_MODEL), row),
        out_shape=jax.ShapeDtypeStruct((n, D_MODEL), F32),
        compiler_params=_cparams(1),
        name="out_mlp",
    )(x2d, o, wo, g, w1, w2)


DSA_TAIL = D_MODEL // 4 + 2 * D_MODEL


def _proj_dsa_kernel(x_ref, g_ref, w_ref, cqg_ref, wuq_ref, wui_ref, qg_ref, kg_ref,
                     c_ref, s1_ref, s2_ref, q_out, qi_out, k_out, v_out, ki_out, wi_out):
    hb = (_rms_rows(x_ref[...]) * g_ref[...]).astype(BF16)
    gmat = _group_ones(256)
    c, s1, s2 = c_ref[...], s1_ref[...], s2_ref[...]

    cq = jnp.dot(hb, w_ref[:, 0:DSA_Q_RANK], preferred_element_type=F32)
    cqb = (_rms_rows(cq) * cqg_ref[...]).astype(BF16)
    qg, kg = qg_ref[...], kg_ref[...]
    for j in range(4):
        pc = jnp.dot(cqb, wuq_ref[:, j * 256:(j + 1) * 256], preferred_element_type=F32)
        _norm_rope_store(pc, gmat, qg, c, s1, s2, Q_SCALE, q_out, j * 256)
    for j in range(2):
        pc = jnp.dot(cqb, wui_ref[:, j * 256:(j + 1) * 256], preferred_element_type=F32)
        _norm_rope_store(pc, None, None, c, s1, s2, 1.0, qi_out, j * 256)
    for j in range(4):
        col = DSA_Q_RANK + j * 256
        pc = jnp.dot(hb, w_ref[:, col:col + 256], preferred_element_type=F32)
        _norm_rope_store(pc, gmat, kg, c, s1, s2, 1.0, k_out, j * 256)
    for j in range(4):
        col = DSA_Q_RANK + D_MODEL + j * 256
        pc = jnp.dot(hb, w_ref[:, col:col + 256], preferred_element_type=F32)
        v_out[:, j * 256:(j + 1) * 256] = pc.astype(BF16)

    t = jnp.dot(hb, w_ref[:, DSA_TAIL:DSA_TAIL + LANES], preferred_element_type=F32)
    lane = lax.broadcasted_iota(jnp.int32, t.shape, 1)
    is_k = lane < IDX_DIM
    ss = jnp.sum(jnp.where(is_k, t * t, 0.0), axis=-1, keepdims=True)
    kn = t * lax.rsqrt(ss * (1.0 / IDX_DIM) + NORM_EPS)
    kr = _rope128(kn, c, s1, s2)
    ki_out[...] = jnp.where(is_k, kr, pltpu.roll(kr, IDX_DIM, 1)).astype(BF16)
    wi_out[...] = t * ((IDX_HEADS ** -0.5) * (IDX_DIM ** -0.5))


def _proj_dsa(x2d, g, w, cqg, wuq, wui, qg, kg, c, s1, s2, tm=256):
    n = x2d.shape[0]
    row = lambda i: (i, 0)
    fixed = lambda i: (0, 0)
    wcols = w.shape[1]
    idx_w = IDX_HEADS * IDX_DIM
    return pl.pallas_call(
        _proj_dsa_kernel,
        grid=(n // tm,),
        in_specs=[
            pl.BlockSpec((tm, D_MODEL), row),
            pl.BlockSpec((1, D_MODEL), fixed),
            pl.BlockSpec((D_MODEL, wcols), fixed),
            pl.BlockSpec((1, DSA_Q_RANK), fixed),
            pl.BlockSpec((DSA_Q_RANK, D_MODEL), fixed),
            pl.BlockSpec((DSA_Q_RANK, idx_w), fixed),
            pl.BlockSpec((1, 256), fixed),
            pl.BlockSpec((1, 256), fixed),
            pl.BlockSpec((tm, LANES), row),
            pl.BlockSpec((tm, LANES), row),
            pl.BlockSpec((tm, LANES), row),
        ],
        out_specs=[
            pl.BlockSpec((tm, D_MODEL), row),
            pl.BlockSpec((tm, idx_w), row),
            pl.BlockSpec((tm, D_MODEL), row),
            pl.BlockSpec((tm, D_MODEL), row),
            pl.BlockSpec((tm, LANES), row),
            pl.BlockSpec((tm, LANES), row),
        ],
        out_shape=[
            jax.ShapeDtypeStruct((n, D_MODEL), BF16),
            jax.ShapeDtypeStruct((n, idx_w), BF16),
            jax.ShapeDtypeStruct((n, D_MODEL), BF16),
            jax.ShapeDtypeStruct((n, D_MODEL), BF16),
            jax.ShapeDtypeStruct((n, LANES), BF16),
            jax.ShapeDtypeStruct((n, LANES), F32),
        ],
        compiler_params=_cparams(1),
        name="proj_dsa",
    )(x2d, g, w, cqg, wuq, wui, qg, kg, c, s1, s2)


def _sortable_key(x):
    bits = lax.bitcast_convert_type(x, jnp.int32)
    return bits ^ (lax.shift_right_arithmetic(bits, 31) & 0x7FFFFFFF)


def _dsa_kernel(q_ref, qi_ref, wi_ref, ki_ref, k_ref, v_ref, o_ref,
                key_ref, qq_ref, m_ref, l_ref, acc_ref, *, tq, tk):
    i = pl.program_id(1)
    ntiles = (i * tq + tq + tk - 1) // tk
    row0 = i * tq

    wi = wi_ref[...]
    qis, wst = [], []
    for c in range(IDX_HEADS // 2):
        qis.append(_stack_pair(qi_ref[:, c * LANES:(c + 1) * LANES]))
        w0 = wi[:, IDX_DIM + 2 * c: IDX_DIM + 2 * c + 1]
        w1 = wi[:, IDX_DIM + 2 * c + 1: IDX_DIM + 2 * c + 2]
        wst.append(jnp.concatenate([w0, w1], axis=0))

    def score_body(t, carry):
        start = pl.multiple_of(t * tk, tk)
        kd = ki_ref[pl.ds(start, tk), :]
        tot = jnp.zeros((2 * tq, tk), F32)
        for c in range(IDX_HEADS // 2):
            tot = tot + wst[c] * jnp.maximum(_qk(qis[c], kd), 0.0)
        score = tot[:tq] + tot[tq:]
        qc = (row0 + lax.broadcasted_iota(jnp.int32, score.shape, 0)) // CHUNK
        kc = (start + lax.broadcasted_iota(jnp.int32, score.shape, 1)) // CHUNK
        key_ref[:, pl.ds(start, tk)] = jnp.where(kc <= qc, _sortable_key(score), KEY_NEG_INF)
        return carry

    lax.fori_loop(0, ntiles, score_body, 0)

    def bit_body(b, thr):
        cand = thr + lax.shift_left(jnp.int32(1), 31 - b)

        def count_body(t, cnt):
            start = pl.multiple_of(t * tk, tk)
            ge = (key_ref[:, pl.ds(start, tk)] >= cand).astype(jnp.int32)
            for u in range(tk // LANES):
                cnt = cnt + ge[:, u * LANES:(u + 1) * LANES]
            return cnt

        cnt = lax.fori_loop(0, ntiles, count_body, jnp.zeros((tq, LANES), jnp.int32))
        total = jnp.sum(cnt, axis=-1, keepdims=True)
        return jnp.where(total >= TOPK, cand, thr)

    thr = lax.fori_loop(0, 32, bit_body, jnp.full((tq, 1), -(2 ** 31), jnp.int32))
    thr = jnp.maximum(thr, KEY_NEG_INF + 1)

    _flash_init(q_ref, qq_ref, m_ref, l_ref, acc_ref)

    def attn_body(t, carry):
        start = pl.multiple_of(t * tk, tk)
        sel = key_ref[:, pl.ds(start, tk)] >= thr
        bias = jnp.where(sel, 0.0, NEG)
        bias = jnp.concatenate([bias, bias], axis=0)
        _flash_tile(qq_ref, k_ref, v_ref, start, tk, bias, m_ref, l_ref, acc_ref)
        return carry

    lax.fori_loop(0, ntiles, attn_body, 0)

    lane = lax.broadcasted_iota(jnp.int32, (tq, LANES), 1)
    for c in range(PAIRS):
        o = _flash_out(c, l_ref, acc_ref)
        o_ref[:, c * LANES:(c + 1) * LANES] = jnp.where(lane < HEAD_DIM, o[:tq], o[tq:]).astype(o_ref.dtype)


def _dsa_attention(q, qi, wi, ki, k, v, batch, seq, tq=128, tk=512):
    nq = seq // tq
    qmap = lambda b, i: (b * nq + i, 0)
    kvmap = lambda b, i: (b, 0)
    once = pl.Buffered(1)
    kern = functools.partial(_dsa_kernel, tq=tq, tk=tk)
    return pl.pallas_call(
        kern,
        grid=(batch, nq),
        in_specs=[
            pl.BlockSpec((tq, D_MODEL), qmap),
            pl.BlockSpec((tq, IDX_HEADS * IDX_DIM), qmap),
            pl.BlockSpec((tq, LANES), qmap),
            pl.BlockSpec((seq, LANES), kvmap),
            pl.BlockSpec((seq, D_MODEL), kvmap, pipeline_mode=once),
            pl.BlockSpec((seq, D_MODEL), kvmap, pipeline_mode=once),
        ],
        out_specs=pl.BlockSpec((tq, D_MODEL), qmap),
        out_shape=jax.ShapeDtypeStruct((batch * seq, D_MODEL), BF16),
        scratch_shapes=[pltpu.VMEM((tq, seq), jnp.int32)] + _flash_scratch(tq),
        compiler_params=_cparams(2),
        name="dsa_attention",
    )(q, qi, wi, ki, k, v)


def _rope_lane_tables(positions):
    n = positions.size
    inv = ROPE_THETA ** (-jnp.arange(0, ROPE_ROT, 2, dtype=F32) / ROPE_ROT)
    ang = positions.reshape(n, 1).astype(F32) * inv
    cos, sin = jnp.cos(ang), jnp.sin(ang)
    rest = HEAD_DIM - ROPE_ROT
    c = jnp.concatenate([cos, cos, jnp.ones((n, rest), F32)], axis=1)
    s1 = jnp.concatenate([jnp.zeros((n, ROPE_HALF), F32), sin, jnp.zeros((n, rest), F32)], axis=1)
    s2 = jnp.concatenate([-sin, jnp.zeros((n, HEAD_DIM - ROPE_HALF), F32)], axis=1)
    return tuple(jnp.tile(t, (1, LANES // HEAD_DIM)) for t in (c, s1, s2))


def _tile_gain(g, width):
    return jnp.tile(g.astype(F32), width // g.shape[0]).reshape(1, width)


def kernel(x, positions, norm_mix, norm_mlp, mlp_w1, mlp_w2, diff_w_in, diff_q_norm, diff_k_norm, diff_lam_q1, diff_lam_k1, diff_lam_q2, diff_lam_k2, diff_subln, diff_w_out, dsa_w_in, dsa_cq_norm, dsa_w_uq, dsa_w_uq_idx, dsa_q_norm, dsa_k_norm, dsa_w_out):
    batch, seq, d = x.shape
    n = batch * seq
    x2d = x.reshape(n, d)
    c, s1, s2 = _rope_lane_tables(positions)
    row = lambda a: a.astype(F32).reshape(1, -1)

    lam_init = 0.8 - 0.6 * math.exp(-0.3 * 0)
    q, k, v = _proj_diff(x2d, row(norm_mix[0]), diff_w_in[0].astype(BF16),
                         _tile_gain(diff_q_norm[0], 256), _tile_gain(diff_k_norm[0], 256), c, s1, s2)
    o = _diff_attention(q, k, v, row(diff_lam_q1[0]), row(diff_lam_k1[0]), row(diff_lam_q2[0]),
                        row(diff_lam_k2[0]), row(diff_subln[0]), batch, seq, lam_init)
    x2d = _out_mlp(x2d, o, diff_w_out[0].astype(BF16), row(norm_mlp[0]),
                   mlp_w1[0].astype(BF16), mlp_w2[0].astype(BF16))

    w_in = dsa_w_in[0]
    pad = jnp.zeros((d, LANES - IDX_DIM - IDX_HEADS), w_in.dtype)
    w_cat = jnp.concatenate([w_in, pad], axis=1).astype(BF16)
    q, qi, k, v, ki, wi = _proj_dsa(x2d, row(norm_mix[1]), w_cat, row(dsa_cq_norm[0]),
                                    dsa_w_uq[0].astype(BF16), dsa_w_uq_idx[0].astype(BF16),
                                    _tile_gain(dsa_q_norm[0], 256), _tile_gain(dsa_k_norm[0], 256),
                                    c, s1, s2)
    o = _dsa_attention(q, qi, wi, ki, k, v, batch, seq)
    x2d = _out_mlp(x2d, o, dsa_w_out[0].astype(BF16), row(norm_mlp[1]),
                   mlp_w1[1].astype(BF16), mlp_w2[1].astype(BF16))
    return x2d.reshape(batch, seq, d)
```

```python
import functools
import math

import numpy as np
import jax
import jax.numpy as jnp
from jax import lax
from jax.experimental import pallas as pl
from jax.experimental.pallas import tpu as pltpu

F32 = jnp.float32
BF16 = jnp.bfloat16

D_MODEL = 1024
CHUNK = 64
HEAD_DIM = 64
ROPE_THETA = 500000.0
ROPE_ROT = HEAD_DIM // 4
ROPE_HALF = ROPE_ROT // 2
NORM_EPS = 1e-6
DIFF_HEADS = D_MODEL // (2 * HEAD_DIM)
DSA_HEADS = D_MODEL // HEAD_DIM
DSA_Q_RANK = D_MODEL // 4
IDX_HEADS = 8
IDX_DIM = 64
TOPK = 256
D_FF = 4 * D_MODEL
LANES = 128
PAIRS = D_MODEL // LANES
NEG = -1e30
Q_SCALE = HEAD_DIM ** -0.5 * math.log2(math.e)
VMEM_LIMIT = 56 * 1024 * 1024

PROJ_ROWS = 256
MLP_ROWS = 512
DIFF_TILES = (512, 512)
DSA_TILES = (512, 512)
COUNT_ROWS = 128

KEY_NEG_INF = -(2 ** 31) + 0x7FFFFF


def _cparams(n_axes):
    return pltpu.CompilerParams(dimension_semantics=("arbitrary",) * n_axes,
                                vmem_limit_bytes=VMEM_LIMIT)


def _rms_rows(x):
    return x * lax.rsqrt(jnp.mean(x * x, axis=-1, keepdims=True) + NORM_EPS)


def _split_bf16(x):
    hi = x.astype(BF16)
    lo = (x - hi.astype(F32)).astype(BF16)
    return hi, lo


def _group_ones(n):
    r = lax.broadcasted_iota(jnp.int32, (n, n), 0) // HEAD_DIM
    c = lax.broadcasted_iota(jnp.int32, (n, n), 1) // HEAD_DIM
    return (r == c).astype(BF16)


def _head_rms(x, gmat):
    hi, lo = _split_bf16(x * x)
    ss = (jnp.dot(hi, gmat, preferred_element_type=F32)
          + jnp.dot(lo, gmat, preferred_element_type=F32))
    return x * lax.rsqrt(ss * (1.0 / HEAD_DIM) + NORM_EPS)


def _rope128(y, c, s1, s2):
    return y * c + pltpu.roll(y, ROPE_HALF, 1) * s1 + pltpu.roll(y, LANES - ROPE_HALF, 1) * s2


def _norm_rope_store(pc, gmat, gain, c, s1, s2, scale, out_ref, col0):
    y = pc if gmat is None else _head_rms(pc, gmat) * gain
    for hh in range(2):
        r = _rope128(y[:, hh * LANES:(hh + 1) * LANES], c, s1, s2)
        if scale != 1.0:
            r = r * scale
        out_ref[:, col0 + hh * LANES: col0 + (hh + 1) * LANES] = r.astype(out_ref.dtype)


def _proj_diff_kernel(x_ref, g_ref, w_ref, qg_ref, kg_ref, c_ref, s1_ref, s2_ref,
                      q_out, k_out, v_out):
    hb = (_rms_rows(x_ref[...]) * g_ref[...]).astype(BF16)
    gmat = _group_ones(256)
    c, s1, s2 = c_ref[...], s1_ref[...], s2_ref[...]
    qg, kg = qg_ref[...], kg_ref[...]
    for j in range(4):
        pc = jnp.dot(hb, w_ref[:, j * 256:(j + 1) * 256], preferred_element_type=F32)
        _norm_rope_store(pc, gmat, qg, c, s1, s2, Q_SCALE, q_out, j * 256)
    for j in range(4):
        pc = jnp.dot(hb, w_ref[:, D_MODEL + j * 256: D_MODEL + (j + 1) * 256],
                     preferred_element_type=F32)
        _norm_rope_store(pc, gmat, kg, c, s1, s2, 1.0, k_out, j * 256)
    for j in range(4):
        pc = jnp.dot(hb, w_ref[:, 2 * D_MODEL + j * 256: 2 * D_MODEL + (j + 1) * 256],
                     preferred_element_type=F32)
        v_out[:, j * 256:(j + 1) * 256] = pc.astype(BF16)


def _proj_diff(x2d, g, w, qg, kg, c, s1, s2):
    n, tm = x2d.shape[0], PROJ_ROWS
    row = lambda i: (i, 0)
    fixed = lambda i: (0, 0)
    return pl.pallas_call(
        _proj_diff_kernel,
        grid=(n // tm,),
        in_specs=[
            pl.BlockSpec((tm, D_MODEL), row),
            pl.BlockSpec((1, D_MODEL), fixed),
            pl.BlockSpec((D_MODEL, 3 * D_MODEL), fixed),
            pl.BlockSpec((1, 256), fixed),
            pl.BlockSpec((1, 256), fixed),
            pl.BlockSpec((tm, LANES), row),
            pl.BlockSpec((tm, LANES), row),
            pl.BlockSpec((tm, LANES), row),
        ],
        out_specs=[pl.BlockSpec((tm, D_MODEL), row)] * 3,
        out_shape=[jax.ShapeDtypeStruct((n, D_MODEL), BF16)] * 3,
        compiler_params=_cparams(1),
        name="proj_diff",
    )(x2d, g, w, qg, kg, c, s1, s2)


def _stack_pair(q):
    lane = lax.broadcasted_iota(jnp.int32, q.shape, 1)
    zero = jnp.zeros_like(q)
    return jnp.concatenate([jnp.where(lane < HEAD_DIM, q, zero),
                            jnp.where(lane >= HEAD_DIM, q, zero)], axis=0)


def _qk(qq, k):
    return lax.dot_general(qq, k, (((1,), (1,)), ((), ())), preferred_element_type=F32)


def _flash_init(q_ref, qq_ref, m_ref, l_ref, acc_ref):
    for c in range(PAIRS):
        qq_ref[c] = _stack_pair(q_ref[:, c * LANES:(c + 1) * LANES])
    m_ref[...] = jnp.full(m_ref.shape, NEG, F32)
    l_ref[...] = jnp.zeros(l_ref.shape, F32)
    acc_ref[...] = jnp.zeros(acc_ref.shape, F32)


def _flash_tile(qq_ref, k_ref, v_ref, start, tk, bias, m_ref, l_ref, acc_ref):
    for c in range(PAIRS):
        cols = slice(c * LANES, (c + 1) * LANES)
        s = _qk(qq_ref[c], k_ref[pl.ds(start, tk), cols])
        if bias is not None:
            s = s + bias
        chunks = [s[:, u * LANES:(u + 1) * LANES] for u in range(tk // LANES)]
        m_prev = m_ref[c]
        m_cur = jnp.max(functools.reduce(jnp.maximum, chunks), axis=-1, keepdims=True)
        m_new = jnp.maximum(m_prev, m_cur)
        alpha = jnp.exp2(m_prev - m_new)
        ps = [jnp.exp2(ch - m_new) for ch in chunks]
        l_ref[c] = alpha * l_ref[c] + functools.reduce(jnp.add, ps)
        p = jnp.concatenate(ps, axis=1).astype(BF16)
        acc_ref[c] = alpha * acc_ref[c] + jnp.dot(p, v_ref[pl.ds(start, tk), cols],
                                                   preferred_element_type=F32)
        m_ref[c] = m_new


def _flash_out(c, l_ref, acc_ref):
    return acc_ref[c] / jnp.sum(l_ref[c], axis=-1, keepdims=True)


def _flash_scratch(tq):
    return [
        pltpu.VMEM((PAIRS, 2 * tq, LANES), BF16),
        pltpu.VMEM((PAIRS, 2 * tq, LANES), F32),
        pltpu.VMEM((PAIRS, 2 * tq, LANES), F32),
        pltpu.VMEM((PAIRS, 2 * tq, LANES), F32),
    ]


def _diff_attn_kernel(q_ref, k_ref, v_ref, lq1_ref, lk1_ref, lq2_ref, lk2_ref, sg_ref, o_ref,
                      qq_ref, m_ref, l_ref, acc_ref, *, tq, tk, lam_init):
    i = pl.program_id(1)
    _flash_init(q_ref, qq_ref, m_ref, l_ref, acc_ref)
    n_full = (i * tq) // tk

    def body(t, carry):
        _flash_tile(qq_ref, k_ref, v_ref, pl.multiple_of(t * tk, tk), tk, None, m_ref, l_ref, acc_ref)
        return carry

    lax.fori_loop(0, n_full, body, 0)

    start = pl.multiple_of(n_full * tk, tk)
    qc = (i * tq + lax.broadcasted_iota(jnp.int32, (2 * tq, tk), 0) % tq) // CHUNK
    kc = (start + lax.broadcasted_iota(jnp.int32, (2 * tq, tk), 1)) // CHUNK
    bias = jnp.where(kc <= qc, 0.0, NEG)
    _flash_tile(qq_ref, k_ref, v_ref, start, tk, bias, m_ref, l_ref, acc_ref)

    lam = (jnp.exp(jnp.sum(lq1_ref[...] * lk1_ref[...], axis=-1, keepdims=True))
           - jnp.exp(jnp.sum(lq2_ref[...] * lk2_ref[...], axis=-1, keepdims=True)) + lam_init)
    for c in range(PAIRS):
        o = _flash_out(c, l_ref, acc_ref)
        o = o[:tq] - lam * o[tq:]
        o = _rms_rows(o) * sg_ref[...] * (1.0 - lam_init)
        o_ref[:, c * LANES:(c + 1) * LANES] = o.astype(o_ref.dtype)


def _diff_attention(q, k, v, lq1, lk1, lq2, lk2, sg, batch, seq, lam_init):
    tq, tk = DIFF_TILES
    nq = seq // tq
    qmap = lambda b, i: (b * nq + i, 0)
    kvmap = lambda b, i: (b, 0)
    fixed = lambda b, i: (0, 0)
    once = pl.Buffered(1)
    kern = functools.partial(_diff_attn_kernel, tq=tq, tk=tk, lam_init=lam_init)
    return pl.pallas_call(
        kern,
        grid=(batch, nq),
        in_specs=[
            pl.BlockSpec((tq, D_MODEL), qmap),
            pl.BlockSpec((seq, D_MODEL), kvmap, pipeline_mode=once),
            pl.BlockSpec((seq, D_MODEL), kvmap, pipeline_mode=once),
            pl.BlockSpec((1, HEAD_DIM), fixed),
            pl.BlockSpec((1, HEAD_DIM), fixed),
            pl.BlockSpec((1, HEAD_DIM), fixed),
            pl.BlockSpec((1, HEAD_DIM), fixed),
            pl.BlockSpec((1, LANES), fixed),
        ],
        out_specs=pl.BlockSpec((tq, D_MODEL), qmap),
        out_shape=jax.ShapeDtypeStruct((batch * seq, D_MODEL), BF16),
        scratch_shapes=_flash_scratch(tq),
        compiler_params=_cparams(2),
        name="diff_attention",
    )(q, k, v, lq1, lk1, lq2, lk2, sg)


def _out_mlp_kernel(x_ref, o_ref, wo_ref, g_ref, w1_ref, w2_ref, out_ref):
    x1 = x_ref[...] + jnp.dot(o_ref[...], wo_ref[...], preferred_element_type=F32)
    hn = (_rms_rows(x1) * g_ref[...]).astype(BF16)
    acc = x1
    for j in range(D_FF // D_MODEL):
        u = jnp.dot(hn, w1_ref[:, j * D_MODEL:(j + 1) * D_MODEL], preferred_element_type=F32)
        u = jnp.maximum(u, 0.0)
        acc = acc + jnp.dot((u * u).astype(BF16), w2_ref[j * D_MODEL:(j + 1) * D_MODEL, :],
                            preferred_element_type=F32)
    out_ref[...] = acc


def _out_mlp(x2d, o, wo, g, w1, w2):
    n, tm = x2d.shape[0], MLP_ROWS
    row = lambda i: (i, 0)
    fixed = lambda i: (0, 0)
    once = pl.Buffered(1)
    return pl.pallas_call(
        _out_mlp_kernel,
        grid=(n // tm,),
        in_specs=[
            pl.BlockSpec((tm, D_MODEL), row),
            pl.BlockSpec((tm, D_MODEL), row),
            pl.BlockSpec((D_MODEL, D_MODEL), fixed, pipeline_mode=once),
            pl.BlockSpec((1, D_MODEL), fixed),
            pl.BlockSpec((D_MODEL, D_FF), fixed, pipeline_mode=once),
            pl.BlockSpec((D_FF, D_MODEL), fixed, pipeline_mode=once),
        ],
        out_specs=pl.BlockSpec((tm, D_MODEL), row),
        out_shape=jax.ShapeDtypeStruct((n, D_MODEL), F32),
        compiler_params=_cparams(1),
        name="out_mlp",
    )(x2d, o, wo, g, w1, w2)


DSA_TAIL = D_MODEL // 4 + 2 * D_MODEL


def _proj_dsa_kernel(x_ref, g_ref, w_ref, cqg_ref, wuq_ref, wui_ref, qg_ref, kg_ref,
                     c_ref, s1_ref, s2_ref, q_out, qi_out, k_out, v_out, ki_out, wi_out):
    hb = (_rms_rows(x_ref[...]) * g_ref[...]).astype(BF16)
    gmat = _group_ones(256)
    c, s1, s2 = c_ref[...], s1_ref[...], s2_ref[...]

    cq = jnp.dot(hb, w_ref[:, 0:DSA_Q_RANK], preferred_element_type=F32)
    cqb = (_rms_rows(cq) * cqg_ref[...]).astype(BF16)
    qg, kg = qg_ref[...], kg_ref[...]
    for j in range(4):
        pc = jnp.dot(cqb, wuq_ref[:, j * 256:(j + 1) * 256], preferred_element_type=F32)
        _norm_rope_store(pc, gmat, qg, c, s1, s2, Q_SCALE, q_out, j * 256)
    for j in range(2):
        pc = jnp.dot(cqb, wui_ref[:, j * 256:(j + 1) * 256], preferred_element_type=F32)
        _norm_rope_store(pc, None, None, c, s1, s2, 1.0, qi_out, j * 256)
    for j in range(4):
        col = DSA_Q_RANK + j * 256
        pc = jnp.dot(hb, w_ref[:, col:col + 256], preferred_element_type=F32)
        _norm_rope_store(pc, gmat, kg, c, s1, s2, 1.0, k_out, j * 256)
    for j in range(4):
        col = DSA_Q_RANK + D_MODEL + j * 256
        pc = jnp.dot(hb, w_ref[:, col:col + 256], preferred_element_type=F32)
        v_out[:, j * 256:(j + 1) * 256] = pc.astype(BF16)

    t = jnp.dot(hb, w_ref[:, DSA_TAIL:DSA_TAIL + LANES], preferred_element_type=F32)
    lane = lax.broadcasted_iota(jnp.int32, t.shape, 1)
    is_k = lane < IDX_DIM
    ss = jnp.sum(jnp.where(is_k, t * t, 0.0), axis=-1, keepdims=True)
    kn = t * lax.rsqrt(ss * (1.0 / IDX_DIM) + NORM_EPS)
    kr = _rope128(kn, c, s1, s2)
    ki_out[...] = jnp.where(is_k, kr, pltpu.roll(kr, IDX_DIM, 1)).astype(BF16)
    wi_out[...] = t * ((IDX_HEADS ** -0.5) * (IDX_DIM ** -0.5))


def _proj_dsa(x2d, g, w, cqg, wuq, wui, qg, kg, c, s1, s2):
    n, tm = x2d.shape[0], PROJ_ROWS
    row = lambda i: (i, 0)
    fixed = lambda i: (0, 0)
    wcols = w.shape[1]
    idx_w = IDX_HEADS * IDX_DIM
    return pl.pallas_call(
        _proj_dsa_kernel,
        grid=(n // tm,),
        in_specs=[
            pl.BlockSpec((tm, D_MODEL), row),
            pl.BlockSpec((1, D_MODEL), fixed),
            pl.BlockSpec((D_MODEL, wcols), fixed),
            pl.BlockSpec((1, DSA_Q_RANK), fixed),
            pl.BlockSpec((DSA_Q_RANK, D_MODEL), fixed),
            pl.BlockSpec((DSA_Q_RANK, idx_w), fixed),
            pl.BlockSpec((1, 256), fixed),
            pl.BlockSpec((1, 256), fixed),
            pl.BlockSpec((tm, LANES), row),
            pl.BlockSpec((tm, LANES), row),
            pl.BlockSpec((tm, LANES), row),
        ],
        out_specs=[
            pl.BlockSpec((tm, D_MODEL), row),
            pl.BlockSpec((tm, idx_w), row),
            pl.BlockSpec((tm, D_MODEL), row),
            pl.BlockSpec((tm, D_MODEL), row),
            pl.BlockSpec((tm, LANES), row),
            pl.BlockSpec((tm, LANES), row),
        ],
        out_shape=[
            jax.ShapeDtypeStruct((n, D_MODEL), BF16),
            jax.ShapeDtypeStruct((n, idx_w), BF16),
            jax.ShapeDtypeStruct((n, D_MODEL), BF16),
            jax.ShapeDtypeStruct((n, D_MODEL), BF16),
            jax.ShapeDtypeStruct((n, LANES), BF16),
            jax.ShapeDtypeStruct((n, LANES), F32),
        ],
        compiler_params=_cparams(1),
        name="proj_dsa",
    )(x2d, g, w, cqg, wuq, wui, qg, kg, c, s1, s2)


def _sortable_key(x):
    bits = lax.bitcast_convert_type(x, jnp.int32)
    return bits ^ (lax.shift_right_arithmetic(bits, 31) & 0x7FFFFFFF)


def _visible_tiles(i, tq, tk):
    return (i * tq + tq + tk - 1) // tk


def _indexer_keys(qi_ref, wi_ref, ki_ref, key_ref, qis_ref, wrep_ref, row0, ntiles, tq, tk):
    wi = wi_ref[...]
    for c in range(IDX_HEADS // 2):
        qis_ref[c] = _stack_pair(qi_ref[:, c * LANES:(c + 1) * LANES])
        w0 = wi[:, IDX_DIM + 2 * c: IDX_DIM + 2 * c + 1]
        w1 = wi[:, IDX_DIM + 2 * c + 1: IDX_DIM + 2 * c + 2]
        wrep_ref[c] = jnp.broadcast_to(jnp.concatenate([w0, w1], axis=0), (2 * tq, LANES))
    qc = (row0 + lax.broadcasted_iota(jnp.int32, (tq, LANES), 0)) // CHUNK
    lane = lax.broadcasted_iota(jnp.int32, (tq, LANES), 1)

    def body(t, carry):
        start = pl.multiple_of(t * tk, tk)
        kd = ki_ref[pl.ds(start, tk), :]
        score = [None] * (tk // LANES)
        for c in range(IDX_HEADS // 2):
            d = _qk(qis_ref[c], kd)
            w = wrep_ref[c]
            for u in range(tk // LANES):
                r = jnp.maximum(d[:, u * LANES:(u + 1) * LANES], 0.0) * w
                r = r[:tq] + r[tq:]
                score[u] = r if score[u] is None else score[u] + r
        keys = []
        for u in range(tk // LANES):
            kc = (start + u * LANES + lane) // CHUNK
            keys.append(jnp.where(kc <= qc, _sortable_key(score[u]), KEY_NEG_INF))
        key_ref[:, pl.ds(start, tk)] = jnp.concatenate(keys, axis=1)
        return carry

    lax.fori_loop(0, ntiles, body, 0)


def _count_rows(key_ref, ntiles, tq, tk, pred, *row_vals):
    parts = []
    for r in range(tq // COUNT_ROWS):
        rows = slice(r * COUNT_ROWS, (r + 1) * COUNT_ROWS)
        vals = [a[rows] for a in row_vals]

        def body(t, cnt, rows=rows, vals=vals):
            start = pl.multiple_of(t * tk, tk)
            kt = key_ref[rows, pl.ds(start, tk)]
            for u in range(tk // LANES):
                hit = pred(kt[:, u * LANES:(u + 1) * LANES], start + u * LANES, *vals)
                cnt = cnt + hit.astype(jnp.int32)
            return cnt

        parts.append(lax.fori_loop(0, ntiles, body, jnp.zeros((COUNT_ROWS, LANES), jnp.int32)))
    total = jnp.sum(jnp.concatenate(parts, axis=0), axis=-1, keepdims=True)
    return jnp.broadcast_to(total, (tq, LANES))


def _topk_threshold(key_ref, ntiles, tq, tk):
    def bit_body(b, carry):
        thr, cnt_thr = carry
        cand = thr + lax.shift_left(jnp.int32(1), 31 - b)
        total = _count_rows(key_ref, ntiles, tq, tk, lambda kt, col, c: kt >= c, cand)
        keep = total >= TOPK
        return jnp.where(keep, cand, thr), jnp.where(keep, total, cnt_thr)

    thr0 = jnp.full((tq, LANES), -(2 ** 31), jnp.int32)
    cnt0 = jnp.zeros((tq, LANES), jnp.int32) + ntiles * tk
    thr, cnt_thr = lax.fori_loop(0, 32, bit_body, (thr0, cnt0))
    few = thr <= KEY_NEG_INF
    return jnp.where(few, KEY_NEG_INF + 1, thr), jnp.where(few, 0, cnt_thr - TOPK)


def _break_ties(key_ref, thr, ntiles, tq, tk, seq):
    def col_of(kt, col0):
        return col0 + lax.broadcasted_iota(jnp.int32, kt.shape, 1)

    above = _count_rows(key_ref, ntiles, tq, tk, lambda kt, col, th: kt > th, thr)
    need = TOPK - above
    nbits = (seq - 1).bit_length()

    def bit_body(b, cut):
        x = cut + lax.shift_left(jnp.int32(1), nbits - 1 - b)
        cnt = _count_rows(key_ref, ntiles, tq, tk,
                          lambda kt, col, th, xx: (kt == th) & (col_of(kt, col) < xx), thr, x)
        return jnp.where(cnt < need, x, cut)

    cut = lax.fori_loop(0, nbits, bit_body, jnp.zeros((tq, LANES), jnp.int32))

    def demote(t, carry):
        start = pl.multiple_of(t * tk, tk)
        kt = key_ref[:, pl.ds(start, tk)]
        out = []
        for u in range(tk // LANES):
            ku = kt[:, u * LANES:(u + 1) * LANES]
            drop = (ku == thr) & (col_of(ku, start + u * LANES) > cut)
            out.append(jnp.where(drop, thr - 1, ku))
        key_ref[:, pl.ds(start, tk)] = jnp.concatenate(out, axis=1)
        return carry

    lax.fori_loop(0, ntiles, demote, 0)


def _dsa_kernel(qblk_ref, tile_ref, last_ref, q_ref, qi_ref, wi_ref, ki_ref, k_ref, v_ref, o_ref,
                key_ref, thr_ref, qis_ref, wrep_ref, qq_ref, m_ref, l_ref, acc_ref, *, tq, tk, seq):
    step = pl.program_id(1)
    i = qblk_ref[step]
    t = tile_ref[step]

    @pl.when(t == 0)
    def _():
        ntiles = _visible_tiles(i, tq, tk)
        _indexer_keys(qi_ref, wi_ref, ki_ref, key_ref, qis_ref, wrep_ref, i * tq, ntiles, tq, tk)
        thr, surplus = _topk_threshold(key_ref, ntiles, tq, tk)

        @pl.when(jnp.max(surplus.astype(F32)) > 0.0)
        def _():
            _break_ties(key_ref, thr, ntiles, tq, tk, seq)

        thr_ref[...] = thr
        _flash_init(q_ref, qq_ref, m_ref, l_ref, acc_ref)

    start = pl.multiple_of(t * tk, tk)
    thr = thr_ref[...]
    kt = key_ref[:, pl.ds(start, tk)]
    bias = jnp.concatenate([jnp.where(kt[:, u * LANES:(u + 1) * LANES] >= thr, 0.0, NEG)
                            for u in range(tk // LANES)], axis=1)
    bias = jnp.concatenate([bias, bias], axis=0)
    _flash_tile(qq_ref, k_ref, v_ref, 0, tk, bias, m_ref, l_ref, acc_ref)

    @pl.when(last_ref[step] == 1)
    def _():
        lane = lax.broadcasted_iota(jnp.int32, (tq, LANES), 1)
        for c in range(PAIRS):
            o = _flash_out(c, l_ref, acc_ref)
            o_ref[:, c * LANES:(c + 1) * LANES] = jnp.where(lane < HEAD_DIM, o[:tq], o[tq:]).astype(o_ref.dtype)


def _dsa_schedule(seq, tq, tk):
    qblk, tile, last = [], [], []
    for i in range(seq // tq):
        nt = _visible_tiles(i, tq, tk)
        for t in range(nt):
            qblk.append(i)
            tile.append(t)
            last.append(int(t == nt - 1))
    return tuple(jnp.asarray(np.asarray(a, np.int32)) for a in (qblk, tile, last))


def _dsa_attention(q, qi, wi, ki, k, v, batch, seq):
    tq, tk = DSA_TILES
    nq, nk = seq // tq, seq // tk
    qblk, tile, last = _dsa_schedule(seq, tq, tk)
    qmap = lambda b, s, qb, tl, ls: (b * nq + qb[s], 0)
    kvmap = lambda b, s, qb, tl, ls: (b * nk + tl[s], 0)
    bmap = lambda b, s, qb, tl, ls: (b, 0)
    kern = functools.partial(_dsa_kernel, tq=tq, tk=tk, seq=seq)
    idx_pairs = IDX_HEADS // 2
    grid_spec = pltpu.PrefetchScalarGridSpec(
        num_scalar_prefetch=3,
        grid=(batch, int(qblk.shape[0])),
        in_specs=[
            pl.BlockSpec((tq, D_MODEL), qmap),
            pl.BlockSpec((tq, IDX_HEADS * IDX_DIM), qmap),
            pl.BlockSpec((tq, LANES), qmap),
            pl.BlockSpec((seq, LANES), bmap),
            pl.BlockSpec((tk, D_MODEL), kvmap),
            pl.BlockSpec((tk, D_MODEL), kvmap),
        ],
        out_specs=pl.BlockSpec((tq, D_MODEL), qmap),
        scratch_shapes=[
            pltpu.VMEM((tq, seq), jnp.int32),
            pltpu.VMEM((tq, LANES), jnp.int32),
            pltpu.VMEM((idx_pairs, 2 * tq, LANES), BF16),
            pltpu.VMEM((idx_pairs, 2 * tq, LANES), F32),
        ] + _flash_scratch(tq),
    )
    return pl.pallas_call(
        kern,
        grid_spec=grid_spec,
        out_shape=jax.ShapeDtypeStruct((batch * seq, D_MODEL), BF16),
        compiler_params=_cparams(2),
        name="dsa_attention",
    )(qblk, tile, last, q, qi, wi, ki, k, v)


def _rope_lane_tables(positions):
    n = positions.size
    inv = ROPE_THETA ** (-jnp.arange(0, ROPE_ROT, 2, dtype=F32) / ROPE_ROT)
    ang = positions.reshape(n, 1).astype(F32) * inv
    cos, sin = jnp.cos(ang), jnp.sin(ang)
    rest = HEAD_DIM - ROPE_ROT
    c = jnp.concatenate([cos, cos, jnp.ones((n, rest), F32)], axis=1)
    s1 = jnp.concatenate([jnp.zeros((n, ROPE_HALF), F32), sin, jnp.zeros((n, rest), F32)], axis=1)
    s2 = jnp.concatenate([-sin, jnp.zeros((n, HEAD_DIM - ROPE_HALF), F32)], axis=1)
    return tuple(jnp.tile(t, (1, LANES // HEAD_DIM)) for t in (c, s1, s2))


def _tile_gain(g, width):
    return jnp.tile(g.astype(F32), width // g.shape[0]).reshape(1, width)


def kernel(x, positions, norm_mix, norm_mlp, mlp_w1, mlp_w2, diff_w_in, diff_q_norm, diff_k_norm, diff_lam_q1, diff_lam_k1, diff_lam_q2, diff_lam_k2, diff_subln, diff_w_out, dsa_w_in, dsa_cq_norm, dsa_w_uq, dsa_w_uq_idx, dsa_q_norm, dsa_k_norm, dsa_w_out):
    batch, seq, d = x.shape
    n = batch * seq
    x2d = x.reshape(n, d)
    c, s1, s2 = _rope_lane_tables(positions)
    row = lambda a: a.astype(F32).reshape(1, -1)

    lam_init = 0.8 - 0.6 * math.exp(-0.3 * 0)
    q, k, v = _proj_diff(x2d, row(norm_mix[0]), diff_w_in[0].astype(BF16),
                         _tile_gain(diff_q_norm[0], 256), _tile_gain(diff_k_norm[0], 256), c, s1, s2)
    o = _diff_attention(q, k, v, row(diff_lam_q1[0]), row(diff_lam_k1[0]), row(diff_lam_q2[0]),
                        row(diff_lam_k2[0]), row(diff_subln[0]), batch, seq, lam_init)
    x2d = _out_mlp(x2d, o, diff_w_out[0].astype(BF16), row(norm_mlp[0]),
                   mlp_w1[0].astype(BF16), mlp_w2[0].astype(BF16))

    w_in = dsa_w_in[0]
    pad = jnp.zeros((d, LANES - IDX_DIM - IDX_HEADS), w_in.dtype)
    w_cat = jnp.concatenate([w_in, pad], axis=1).astype(BF16)
    q, qi, k, v, ki, wi = _proj_dsa(x2d, row(norm_mix[1]), w_cat, row(dsa_cq_norm[0]),
                                    dsa_w_uq[0].astype(BF16), dsa_w_uq_idx[0].astype(BF16),
                                    _tile_gain(dsa_q_norm[0], 256), _tile_gain(dsa_k_norm[0], 256),
                                    c, s1, s2)
    o = _dsa_attention(q, qi, wi, ki, k, v, batch, seq)
    x2d = _out_mlp(x2d, o, dsa_w_out[0].astype(BF16), row(norm_mlp[1]),
                   mlp_w1[1].astype(BF16), mlp_w2[1].astype(BF16))
    return x2d.reshape(batch, seq, d)
```

```python
import functools
import math

import numpy as np
import jax
import jax.numpy as jnp
from jax import lax
from jax.experimental import pallas as pl
from jax.experimental.pallas import tpu as pltpu

F32 = jnp.float32
BF16 = jnp.bfloat16
I16 = jnp.int16
I16_MIN, I16_MAX = -(2 ** 15), 2 ** 15 - 1

D_MODEL = 1024
CHUNK = 64
HEAD_DIM = 64
ROPE_THETA = 500000.0
ROPE_ROT = HEAD_DIM // 4
ROPE_HALF = ROPE_ROT // 2
NORM_EPS = 1e-6
DIFF_HEADS = D_MODEL // (2 * HEAD_DIM)
DSA_HEADS = D_MODEL // HEAD_DIM
DSA_Q_RANK = D_MODEL // 4
IDX_HEADS = 8
IDX_DIM = 64
TOPK = 256
D_FF = 4 * D_MODEL
LANES = 128
PAIRS = D_MODEL // LANES
NEG = -1e30
Q_SCALE = HEAD_DIM ** -0.5 * math.log2(math.e)
VMEM_LIMIT = 56 * 1024 * 1024

PROJ_ROWS = 512
PROJ_COLS = 512
MLP_ROWS = 512
DIFF_TILES = (512, 512)
DSA_TILES = (512, 512)
COUNT_ROWS = 256

KEY_NEG_INF = -(2 ** 31) + 0x7FFFFF
HI_NEG_INF = KEY_NEG_INF >> 16


def _cparams(n_axes):
    return pltpu.CompilerParams(dimension_semantics=("arbitrary",) * n_axes,
                                vmem_limit_bytes=VMEM_LIMIT)


def _rms_rows(x):
    return x * lax.rsqrt(jnp.mean(x * x, axis=-1, keepdims=True) + NORM_EPS)


def _split_bf16(x):
    hi = x.astype(BF16)
    lo = (x - hi.astype(F32)).astype(BF16)
    return hi, lo


def _group_ones(n):
    r = lax.broadcasted_iota(jnp.int32, (n, n), 0) // HEAD_DIM
    c = lax.broadcasted_iota(jnp.int32, (n, n), 1) // HEAD_DIM
    return (r == c).astype(BF16)


def _head_rms(x, gmat):
    hi, lo = _split_bf16(x * x)
    ss = (jnp.dot(hi, gmat, preferred_element_type=F32)
          + jnp.dot(lo, gmat, preferred_element_type=F32))
    return x * lax.rsqrt(ss * (1.0 / HEAD_DIM) + NORM_EPS)


def _rope128(y, c, s1, s2):
    return y * c + pltpu.roll(y, ROPE_HALF, 1) * s1 + pltpu.roll(y, LANES - ROPE_HALF, 1) * s2


def _norm_rope_store(pc, gmat, gain, c, s1, s2, scale, out_ref, col0):
    y = pc if gmat is None else _head_rms(pc, gmat) * gain
    for hh in range(2):
        r = _rope128(y[:, hh * LANES:(hh + 1) * LANES], c, s1, s2)
        if scale != 1.0:
            r = r * scale
        out_ref[:, col0 + hh * LANES: col0 + (hh + 1) * LANES] = r.astype(out_ref.dtype)


def _proj_cols(hb, w_ref, col0, width):
    blocks = []
    for j in range(0, width, PROJ_COLS):
        w = min(PROJ_COLS, width - j)
        pc = jnp.dot(hb, w_ref[:, col0 + j: col0 + j + w], preferred_element_type=F32)
        blocks += [pc[:, i:i + 256] for i in range(0, w, 256)]
    return blocks


def _proj_diff_kernel(x_ref, g_ref, w_ref, qg_ref, kg_ref, c_ref, s1_ref, s2_ref,
                      q_out, k_out, v_out):
    hb = (_rms_rows(x_ref[...]) * g_ref[...]).astype(BF16)
    gmat = _group_ones(256)
    c, s1, s2 = c_ref[...], s1_ref[...], s2_ref[...]
    qg, kg = qg_ref[...], kg_ref[...]
    for j, pc in enumerate(_proj_cols(hb, w_ref, 0, D_MODEL)):
        _norm_rope_store(pc, gmat, qg, c, s1, s2, Q_SCALE, q_out, j * 256)
    for j, pc in enumerate(_proj_cols(hb, w_ref, D_MODEL, D_MODEL)):
        _norm_rope_store(pc, gmat, kg, c, s1, s2, 1.0, k_out, j * 256)
    for j, pc in enumerate(_proj_cols(hb, w_ref, 2 * D_MODEL, D_MODEL)):
        v_out[:, j * 256:(j + 1) * 256] = pc.astype(BF16)


def _proj_diff(x2d, g, w, qg, kg, c, s1, s2):
    n, tm = x2d.shape[0], PROJ_ROWS
    row = lambda i: (i, 0)
    fixed = lambda i: (0, 0)
    return pl.pallas_call(
        _proj_diff_kernel,
        grid=(n // tm,),
        in_specs=[
            pl.BlockSpec((tm, D_MODEL), row),
            pl.BlockSpec((1, D_MODEL), fixed),
            pl.BlockSpec((D_MODEL, 3 * D_MODEL), fixed),
            pl.BlockSpec((1, 256), fixed),
            pl.BlockSpec((1, 256), fixed),
            pl.BlockSpec((tm, LANES), row),
            pl.BlockSpec((tm, LANES), row),
            pl.BlockSpec((tm, LANES), row),
        ],
        out_specs=[pl.BlockSpec((tm, D_MODEL), row)] * 3,
        out_shape=[jax.ShapeDtypeStruct((n, D_MODEL), BF16)] * 3,
        compiler_params=_cparams(1),
        name="proj_diff",
    )(x2d, g, w, qg, kg, c, s1, s2)


def _stack_pair(q):
    lane = lax.broadcasted_iota(jnp.int32, q.shape, 1)
    zero = jnp.zeros_like(q)
    return jnp.concatenate([jnp.where(lane < HEAD_DIM, q, zero),
                            jnp.where(lane >= HEAD_DIM, q, zero)], axis=0)


def _qk(qq, k):
    return lax.dot_general(qq, k, (((1,), (1,)), ((), ())), preferred_element_type=F32)


def _flash_init(q_ref, qq_ref, m_ref, l_ref, acc_ref):
    for c in range(PAIRS):
        qq_ref[c] = _stack_pair(q_ref[:, c * LANES:(c + 1) * LANES])
    m_ref[...] = jnp.full(m_ref.shape, NEG, F32)
    l_ref[...] = jnp.zeros(l_ref.shape, F32)
    acc_ref[...] = jnp.zeros(acc_ref.shape, F32)


def _flash_tile(qq_ref, k_ref, v_ref, start, tk, bias, m_ref, l_ref, acc_ref):
    for c in range(PAIRS):
        cols = slice(c * LANES, (c + 1) * LANES)
        s = _qk(qq_ref[c], k_ref[pl.ds(start, tk), cols])
        if bias is not None:
            s = s + bias
        chunks = [s[:, u * LANES:(u + 1) * LANES] for u in range(tk // LANES)]
        m_prev = m_ref[c]
        m_cur = jnp.max(functools.reduce(jnp.maximum, chunks), axis=-1, keepdims=True)
        m_new = jnp.maximum(m_prev, m_cur)
        alpha = jnp.exp2(m_prev - m_new)
        ps = [jnp.exp2(ch - m_new) for ch in chunks]
        l_ref[c] = alpha * l_ref[c] + functools.reduce(jnp.add, ps)
        p = jnp.concatenate(ps, axis=1).astype(BF16)
        acc_ref[c] = alpha * acc_ref[c] + jnp.dot(p, v_ref[pl.ds(start, tk), cols],
                                                   preferred_element_type=F32)
        m_ref[c] = m_new


def _flash_out(c, l_ref, acc_ref):
    return acc_ref[c] / jnp.sum(l_ref[c], axis=-1, keepdims=True)


def _flash_scratch(tq):
    return [
        pltpu.VMEM((PAIRS, 2 * tq, LANES), BF16),
        pltpu.VMEM((PAIRS, 2 * tq, LANES), F32),
        pltpu.VMEM((PAIRS, 2 * tq, LANES), F32),
        pltpu.VMEM((PAIRS, 2 * tq, LANES), F32),
    ]


def _diff_attn_kernel(q_ref, k_ref, v_ref, lq1_ref, lk1_ref, lq2_ref, lk2_ref, sg_ref, o_ref,
                      qq_ref, m_ref, l_ref, acc_ref, *, tq, tk, lam_init):
    i = pl.program_id(1)
    _flash_init(q_ref, qq_ref, m_ref, l_ref, acc_ref)
    n_full = (i * tq) // tk

    def body(t, carry):
        _flash_tile(qq_ref, k_ref, v_ref, pl.multiple_of(t * tk, tk), tk, None, m_ref, l_ref, acc_ref)
        return carry

    lax.fori_loop(0, n_full, body, 0)

    start = pl.multiple_of(n_full * tk, tk)
    qc = (i * tq + lax.broadcasted_iota(jnp.int32, (2 * tq, tk), 0) % tq) // CHUNK
    kc = (start + lax.broadcasted_iota(jnp.int32, (2 * tq, tk), 1)) // CHUNK
    bias = jnp.where(kc <= qc, 0.0, NEG)
    _flash_tile(qq_ref, k_ref, v_ref, start, tk, bias, m_ref, l_ref, acc_ref)

    lam = (jnp.exp(jnp.sum(lq1_ref[...] * lk1_ref[...], axis=-1, keepdims=True))
           - jnp.exp(jnp.sum(lq2_ref[...] * lk2_ref[...], axis=-1, keepdims=True)) + lam_init)
    for c in range(PAIRS):
        o = _flash_out(c, l_ref, acc_ref)
        o = o[:tq] - lam * o[tq:]
        o = _rms_rows(o) * sg_ref[...] * (1.0 - lam_init)
        o_ref[:, c * LANES:(c + 1) * LANES] = o.astype(o_ref.dtype)


def _diff_attention(q, k, v, lq1, lk1, lq2, lk2, sg, batch, seq, lam_init):
    tq, tk = DIFF_TILES
    nq = seq // tq
    qmap = lambda b, i: (b * nq + i, 0)
    kvmap = lambda b, i: (b, 0)
    fixed = lambda b, i: (0, 0)
    once = pl.Buffered(1)
    kern = functools.partial(_diff_attn_kernel, tq=tq, tk=tk, lam_init=lam_init)
    return pl.pallas_call(
        kern,
        grid=(batch, nq),
        in_specs=[
            pl.BlockSpec((tq, D_MODEL), qmap),
            pl.BlockSpec((seq, D_MODEL), kvmap, pipeline_mode=once),
            pl.BlockSpec((seq, D_MODEL), kvmap, pipeline_mode=once),
            pl.BlockSpec((1, HEAD_DIM), fixed),
            pl.BlockSpec((1, HEAD_DIM), fixed),
            pl.BlockSpec((1, HEAD_DIM), fixed),
            pl.BlockSpec((1, HEAD_DIM), fixed),
            pl.BlockSpec((1, LANES), fixed),
        ],
        out_specs=pl.BlockSpec((tq, D_MODEL), qmap),
        out_shape=jax.ShapeDtypeStruct((batch * seq, D_MODEL), BF16),
        scratch_shapes=_flash_scratch(tq),
        compiler_params=_cparams(2),
        name="diff_attention",
    )(q, k, v, lq1, lk1, lq2, lk2, sg)


def _out_mlp_kernel(x_ref, o_ref, wo_ref, g_ref, w1_ref, w2_ref, out_ref):
    x1 = x_ref[...] + jnp.dot(o_ref[...], wo_ref[...], preferred_element_type=F32)
    hn = (_rms_rows(x1) * g_ref[...]).astype(BF16)
    acc = x1
    for j in range(D_FF // D_MODEL):
        u = jnp.dot(hn, w1_ref[:, j * D_MODEL:(j + 1) * D_MODEL], preferred_element_type=F32)
        u = jnp.maximum(u, 0.0)
        acc = acc + jnp.dot((u * u).astype(BF16), w2_ref[j * D_MODEL:(j + 1) * D_MODEL, :],
                            preferred_element_type=F32)
    out_ref[...] = acc


def _out_mlp(x2d, o, wo, g, w1, w2):
    n, tm = x2d.shape[0], MLP_ROWS
    row = lambda i: (i, 0)
    fixed = lambda i: (0, 0)
    once = pl.Buffered(1)
    return pl.pallas_call(
        _out_mlp_kernel,
        grid=(n // tm,),
        in_specs=[
            pl.BlockSpec((tm, D_MODEL), row),
            pl.BlockSpec((tm, D_MODEL), row),
            pl.BlockSpec((D_MODEL, D_MODEL), fixed, pipeline_mode=once),
            pl.BlockSpec((1, D_MODEL), fixed),
            pl.BlockSpec((D_MODEL, D_FF), fixed, pipeline_mode=once),
            pl.BlockSpec((D_FF, D_MODEL), fixed, pipeline_mode=once),
        ],
        out_specs=pl.BlockSpec((tm, D_MODEL), row),
        out_shape=jax.ShapeDtypeStruct((n, D_MODEL), F32),
        compiler_params=_cparams(1),
        name="out_mlp",
    )(x2d, o, wo, g, w1, w2)


DSA_TAIL = D_MODEL // 4 + 2 * D_MODEL


def _proj_dsa_kernel(x_ref, g_ref, w_ref, cqg_ref, wuq_ref, wui_ref, qg_ref, kg_ref,
                     c_ref, s1_ref, s2_ref, q_out, qi_out, k_out, v_out, ki_out, wi_out):
    hb = (_rms_rows(x_ref[...]) * g_ref[...]).astype(BF16)
    gmat = _group_ones(256)
    c, s1, s2 = c_ref[...], s1_ref[...], s2_ref[...]

    cq = jnp.dot(hb, w_ref[:, 0:DSA_Q_RANK], preferred_element_type=F32)
    cqb = (_rms_rows(cq) * cqg_ref[...]).astype(BF16)
    qg, kg = qg_ref[...], kg_ref[...]
    for j, pc in enumerate(_proj_cols(cqb, wuq_ref, 0, D_MODEL)):
        _norm_rope_store(pc, gmat, qg, c, s1, s2, Q_SCALE, q_out, j * 256)
    for j, pc in enumerate(_proj_cols(cqb, wui_ref, 0, IDX_HEADS * IDX_DIM)):
        _norm_rope_store(pc, None, None, c, s1, s2, 1.0, qi_out, j * 256)
    for j, pc in enumerate(_proj_cols(hb, w_ref, DSA_Q_RANK, D_MODEL)):
        _norm_rope_store(pc, gmat, kg, c, s1, s2, 1.0, k_out, j * 256)
    for j, pc in enumerate(_proj_cols(hb, w_ref, DSA_Q_RANK + D_MODEL, D_MODEL)):
        v_out[:, j * 256:(j + 1) * 256] = pc.astype(BF16)

    t = jnp.dot(hb, w_ref[:, DSA_TAIL:DSA_TAIL + LANES], preferred_element_type=F32)
    lane = lax.broadcasted_iota(jnp.int32, t.shape, 1)
    is_k = lane < IDX_DIM
    ss = jnp.sum(jnp.where(is_k, t * t, 0.0), axis=-1, keepdims=True)
    kn = t * lax.rsqrt(ss * (1.0 / IDX_DIM) + NORM_EPS)
    kr = _rope128(kn, c, s1, s2)
    ki_out[...] = jnp.where(is_k, kr, pltpu.roll(kr, IDX_DIM, 1)).astype(BF16)
    wi_out[...] = t * ((IDX_HEADS ** -0.5) * (IDX_DIM ** -0.5))


def _proj_dsa(x2d, g, w, cqg, wuq, wui, qg, kg, c, s1, s2):
    n, tm = x2d.shape[0], PROJ_ROWS
    row = lambda i: (i, 0)
    fixed = lambda i: (0, 0)
    wcols = w.shape[1]
    idx_w = IDX_HEADS * IDX_DIM
    return pl.pallas_call(
        _proj_dsa_kernel,
        grid=(n // tm,),
        in_specs=[
            pl.BlockSpec((tm, D_MODEL), row),
            pl.BlockSpec((1, D_MODEL), fixed),
            pl.BlockSpec((D_MODEL, wcols), fixed),
            pl.BlockSpec((1, DSA_Q_RANK), fixed),
            pl.BlockSpec((DSA_Q_RANK, D_MODEL), fixed),
            pl.BlockSpec((DSA_Q_RANK, idx_w), fixed),
            pl.BlockSpec((1, 256), fixed),
            pl.BlockSpec((1, 256), fixed),
            pl.BlockSpec((tm, LANES), row),
            pl.BlockSpec((tm, LANES), row),
            pl.BlockSpec((tm, LANES), row),
        ],
        out_specs=[
            pl.BlockSpec((tm, D_MODEL), row),
            pl.BlockSpec((tm, idx_w), row),
            pl.BlockSpec((tm, D_MODEL), row),
            pl.BlockSpec((tm, D_MODEL), row),
            pl.BlockSpec((tm, LANES), row),
            pl.BlockSpec((tm, LANES), row),
        ],
        out_shape=[
            jax.ShapeDtypeStruct((n, D_MODEL), BF16),
            jax.ShapeDtypeStruct((n, idx_w), BF16),
            jax.ShapeDtypeStruct((n, D_MODEL), BF16),
            jax.ShapeDtypeStruct((n, D_MODEL), BF16),
            jax.ShapeDtypeStruct((n, LANES), BF16),
            jax.ShapeDtypeStruct((n, LANES), F32),
        ],
        compiler_params=_cparams(1),
        name="proj_dsa",
    )(x2d, g, w, cqg, wuq, wui, qg, kg, c, s1, s2)


def _sortable_key(x):
    bits = lax.bitcast_convert_type(x, jnp.int32)
    return bits ^ (lax.shift_right_arithmetic(bits, 31) & 0x7FFFFFFF)


def _visible_tiles(i, tq, tk):
    return (i * tq + tq + tk - 1) // tk


def _split_key(key):
    hi = lax.shift_right_arithmetic(key, 16).astype(I16)
    lo = ((key & 0xFFFF) - 0x8000).astype(I16)
    return hi, lo


def _indexer_keys(qi_ref, wi_ref, ki_ref, khi_ref, klo_ref, qis_ref, wrep_ref, row0, ntiles, tq, tk):
    wi = wi_ref[...]
    for c in range(IDX_HEADS // 2):
        qis_ref[c] = _stack_pair(qi_ref[:, c * LANES:(c + 1) * LANES])
        w0 = wi[:, IDX_DIM + 2 * c: IDX_DIM + 2 * c + 1]
        w1 = wi[:, IDX_DIM + 2 * c + 1: IDX_DIM + 2 * c + 2]
        wrep_ref[c] = jnp.broadcast_to(jnp.concatenate([w0, w1], axis=0), (2 * tq, LANES))
    qc = (row0 + lax.broadcasted_iota(jnp.int32, (tq, LANES), 0)) // CHUNK
    lane = lax.broadcasted_iota(jnp.int32, (tq, LANES), 1)

    def body(t, carry):
        start = pl.multiple_of(t * tk, tk)
        kd = ki_ref[pl.ds(start, tk), :]
        score = [None] * (tk // LANES)
        for c in range(IDX_HEADS // 2):
            d = _qk(qis_ref[c], kd)
            w = wrep_ref[c]
            for u in range(tk // LANES):
                r = jnp.maximum(d[:, u * LANES:(u + 1) * LANES], 0.0) * w
                r = r[:tq] + r[tq:]
                score[u] = r if score[u] is None else score[u] + r
        his, los = [], []
        for u in range(tk // LANES):
            kc = (start + u * LANES + lane) // CHUNK
            hi, lo = _split_key(jnp.where(kc <= qc, _sortable_key(score[u]), KEY_NEG_INF))
            his.append(hi)
            los.append(lo)
        khi_ref[:, pl.ds(start, tk)] = jnp.concatenate(his, axis=1)
        klo_ref[:, pl.ds(start, tk)] = jnp.concatenate(los, axis=1)
        return carry

    lax.fori_loop(0, ntiles, body, 0)


def _count_rows(plane_refs, ntiles, tq, tk, pred, *row_vals):
    one, zero = jnp.ones((), I16), jnp.zeros((), I16)
    parts = []
    for r in range(tq // COUNT_ROWS):
        rows = slice(r * COUNT_ROWS, (r + 1) * COUNT_ROWS)
        vals = [a[rows] for a in row_vals]

        def body(t, cnt, rows=rows, vals=vals):
            start = pl.multiple_of(t * tk, tk)
            tiles = [p[rows, pl.ds(start, tk)] for p in plane_refs]
            for u in range(tk // LANES):
                hit = pred(*[x[:, u * LANES:(u + 1) * LANES] for x in tiles], start + u * LANES, *vals)
                cnt = cnt + jnp.where(hit, one, zero)
            return cnt

        parts.append(lax.fori_loop(0, ntiles, body, jnp.zeros((COUNT_ROWS, LANES), I16)))
    total = jnp.sum(jnp.concatenate(parts, axis=0).astype(jnp.int32), axis=-1, keepdims=True)
    return jnp.broadcast_to(total, (tq, LANES))


def _bit_search16(plane_ref, ntiles, tq, tk, cnt_min):
    def bit_body(b, carry):
        thr, cnt_thr = carry
        cand = thr + lax.shift_left(jnp.int32(1), 15 - b)
        total = _count_rows([plane_ref], ntiles, tq, tk, lambda x, col, c: x >= c, cand.astype(I16))
        keep = total >= TOPK
        return jnp.where(keep, cand, thr), jnp.where(keep, total, cnt_thr)

    return lax.fori_loop(0, 16, bit_body, (jnp.full((tq, LANES), I16_MIN, jnp.int32), cnt_min))


def _topk_threshold(khi_ref, klo_ref, ntiles, tq, tk):
    cnt_all = jnp.zeros((tq, LANES), jnp.int32) + ntiles * tk
    t_hi, cnt_hi = _bit_search16(khi_ref, ntiles, tq, tk, cnt_all)
    few = t_hi <= HI_NEG_INF
    t_hi = jnp.where(few, HI_NEG_INF + 1, t_hi)
    hi16 = t_hi.astype(I16)
    top, bottom = jnp.full((), I16_MAX, I16), jnp.full((), I16_MIN, I16)

    def rewrite_low(t, carry):
        start = pl.multiple_of(t * tk, tk)
        h, l = khi_ref[:, pl.ds(start, tk)], klo_ref[:, pl.ds(start, tk)]
        out = []
        for u in range(tk // LANES):
            hu, lu = h[:, u * LANES:(u + 1) * LANES], l[:, u * LANES:(u + 1) * LANES]
            out.append(jnp.where(hu > hi16, top, jnp.where(hu == hi16, lu, bottom)))
        klo_ref[:, pl.ds(start, tk)] = jnp.concatenate(out, axis=1)
        return carry

    lax.fori_loop(0, ntiles, rewrite_low, 0)
    t_lo, cnt_lo = _bit_search16(klo_ref, ntiles, tq, tk, cnt_hi)
    return t_hi, t_lo, jnp.where(few, 0, cnt_lo - TOPK)


def _break_ties(khi_ref, klo_ref, t_hi, t_lo, ntiles, tq, tk, seq):
    hi16, lo16 = t_hi.astype(I16), t_lo.astype(I16)

    def cols(x, col0):
        return (col0 + lax.broadcasted_iota(jnp.int32, x.shape, 1)).astype(I16)

    above = _count_rows([khi_ref, klo_ref], ntiles, tq, tk,
                        lambda h, l, col, a, b: (h > a) | ((h == a) & (l > b)), hi16, lo16)
    need = TOPK - above
    nbits = (seq - 1).bit_length()

    def bit_body(b, cut):
        x = cut + lax.shift_left(jnp.int32(1), nbits - 1 - b)
        cnt = _count_rows([khi_ref, klo_ref], ntiles, tq, tk,
                          lambda h, l, col, a, b_, xx: (h == a) & (l == b_) & (cols(h, col) < xx),
                          hi16, lo16, x.astype(I16))
        return jnp.where(cnt < need, x, cut)

    cut16 = lax.fori_loop(0, nbits, bit_body, jnp.zeros((tq, LANES), jnp.int32)).astype(I16)
    below = (t_hi - 1).astype(I16)

    def demote(t, carry):
        start = pl.multiple_of(t * tk, tk)
        h, l = khi_ref[:, pl.ds(start, tk)], klo_ref[:, pl.ds(start, tk)]
        out = []
        for u in range(tk // LANES):
            hu, lu = h[:, u * LANES:(u + 1) * LANES], l[:, u * LANES:(u + 1) * LANES]
            drop = (hu == hi16) & (lu == lo16) & (cols(hu, start + u * LANES) > cut16)
            out.append(jnp.where(drop, below, hu))
        khi_ref[:, pl.ds(start, tk)] = jnp.concatenate(out, axis=1)
        return carry

    lax.fori_loop(0, ntiles, demote, 0)


def _dsa_kernel(qblk_ref, tile_ref, last_ref, q_ref, qi_ref, wi_ref, ki_ref, k_ref, v_ref, o_ref,
                khi_ref, klo_ref, thi_ref, tlo_ref, qis_ref, wrep_ref, qq_ref, m_ref, l_ref, acc_ref,
                *, tq, tk, seq):
    step = pl.program_id(1)
    i = qblk_ref[step]
    t = tile_ref[step]

    @pl.when(t == 0)
    def _():
        ntiles = _visible_tiles(i, tq, tk)
        _indexer_keys(qi_ref, wi_ref, ki_ref, khi_ref, klo_ref, qis_ref, wrep_ref, i * tq, ntiles, tq, tk)
        t_hi, t_lo, surplus = _topk_threshold(khi_ref, klo_ref, ntiles, tq, tk)

        @pl.when(jnp.max(surplus.astype(F32)) > 0.0)
        def _():
            _break_ties(khi_ref, klo_ref, t_hi, t_lo, ntiles, tq, tk, seq)

        thi_ref[...] = t_hi.astype(I16)
        tlo_ref[...] = t_lo.astype(I16)
        _flash_init(q_ref, qq_ref, m_ref, l_ref, acc_ref)

    start = pl.multiple_of(t * tk, tk)
    hi16, lo16 = thi_ref[...], tlo_ref[...]
    h, l = khi_ref[:, pl.ds(start, tk)], klo_ref[:, pl.ds(start, tk)]
    keep, drop = jnp.zeros((), BF16), jnp.full((), NEG, BF16)
    bias = jnp.concatenate(
        [jnp.where((l[:, u * LANES:(u + 1) * LANES] >= lo16) & (h[:, u * LANES:(u + 1) * LANES] >= hi16),
                   keep, drop).astype(F32) for u in range(tk // LANES)], axis=1)
    bias = jnp.concatenate([bias, bias], axis=0)
    _flash_tile(qq_ref, k_ref, v_ref, 0, tk, bias, m_ref, l_ref, acc_ref)

    @pl.when(last_ref[step] == 1)
    def _():
        lane = lax.broadcasted_iota(jnp.int32, (tq, LANES), 1)
        for c in range(PAIRS):
            o = _flash_out(c, l_ref, acc_ref)
            o_ref[:, c * LANES:(c + 1) * LANES] = jnp.where(lane < HEAD_DIM, o[:tq], o[tq:]).astype(o_ref.dtype)


def _dsa_schedule(seq, tq, tk):
    qblk, tile, last = [], [], []
    for i in range(seq // tq):
        nt = _visible_tiles(i, tq, tk)
        for t in range(nt):
            qblk.append(i)
            tile.append(t)
            last.append(int(t == nt - 1))
    return tuple(jnp.asarray(np.asarray(a, np.int32)) for a in (qblk, tile, last))


def _dsa_attention(q, qi, wi, ki, k, v, batch, seq):
    tq, tk = DSA_TILES
    assert seq <= I16_MAX, "column indices and per-lane counts are held in int16"
    nq, nk = seq // tq, seq // tk
    qblk, tile, last = _dsa_schedule(seq, tq, tk)
    qmap = lambda b, s, qb, tl, ls: (b * nq + qb[s], 0)
    kvmap = lambda b, s, qb, tl, ls: (b * nk + tl[s], 0)
    bmap = lambda b, s, qb, tl, ls: (b, 0)
    kern = functools.partial(_dsa_kernel, tq=tq, tk=tk, seq=seq)
    idx_pairs = IDX_HEADS // 2
    grid_spec = pltpu.PrefetchScalarGridSpec(
        num_scalar_prefetch=3,
        grid=(batch, int(qblk.shape[0])),
        in_specs=[
            pl.BlockSpec((tq, D_MODEL), qmap),
            pl.BlockSpec((tq, IDX_HEADS * IDX_DIM), qmap),
            pl.BlockSpec((tq, LANES), qmap),
            pl.BlockSpec((seq, LANES), bmap),
            pl.BlockSpec((tk, D_MODEL), kvmap),
            pl.BlockSpec((tk, D_MODEL), kvmap),
        ],
        out_specs=pl.BlockSpec((tq, D_MODEL), qmap),
        scratch_shapes=[
            pltpu.VMEM((tq, seq), I16),
            pltpu.VMEM((tq, seq), I16),
            pltpu.VMEM((tq, LANES), I16),
            pltpu.VMEM((tq, LANES), I16),
            pltpu.VMEM((idx_pairs, 2 * tq, LANES), BF16),
            pltpu.VMEM((idx_pairs, 2 * tq, LANES), F32),
        ] + _flash_scratch(tq),
    )
    return pl.pallas_call(
        kern,
        grid_spec=grid_spec,
        out_shape=jax.ShapeDtypeStruct((batch * seq, D_MODEL), BF16),
        compiler_params=_cparams(2),
        name="dsa_attention",
    )(qblk, tile, last, q, qi, wi, ki, k, v)


def _rope_lane_tables(positions):
    n = positions.size
    inv = ROPE_THETA ** (-jnp.arange(0, ROPE_ROT, 2, dtype=F32) / ROPE_ROT)
    ang = positions.reshape(n, 1).astype(F32) * inv
    cos, sin = jnp.cos(ang), jnp.sin(ang)
    rest = HEAD_DIM - ROPE_ROT
    c = jnp.concatenate([cos, cos, jnp.ones((n, rest), F32)], axis=1)
    s1 = jnp.concatenate([jnp.zeros((n, ROPE_HALF), F32), sin, jnp.zeros((n, rest), F32)], axis=1)
    s2 = jnp.concatenate([-sin, jnp.zeros((n, HEAD_DIM - ROPE_HALF), F32)], axis=1)
    return tuple(jnp.tile(t, (1, LANES // HEAD_DIM)) for t in (c, s1, s2))


def _tile_gain(g, width):
    return jnp.tile(g.astype(F32), width // g.shape[0]).reshape(1, width)


def kernel(x, positions, norm_mix, norm_mlp, mlp_w1, mlp_w2, diff_w_in, diff_q_norm, diff_k_norm, diff_lam_q1, diff_lam_k1, diff_lam_q2, diff_lam_k2, diff_subln, diff_w_out, dsa_w_in, dsa_cq_norm, dsa_w_uq, dsa_w_uq_idx, dsa_q_norm, dsa_k_norm, dsa_w_out):
    batch, seq, d = x.shape
    n = batch * seq
    x2d = x.reshape(n, d)
    c, s1, s2 = _rope_lane_tables(positions)
    row = lambda a: a.astype(F32).reshape(1, -1)

    lam_init = 0.8 - 0.6 * math.exp(-0.3 * 0)
    q, k, v = _proj_diff(x2d, row(norm_mix[0]), diff_w_in[0].astype(BF16),
                         _tile_gain(diff_q_norm[0], 256), _tile_gain(diff_k_norm[0], 256), c, s1, s2)
    o = _diff_attention(q, k, v, row(diff_lam_q1[0]), row(diff_lam_k1[0]), row(diff_lam_q2[0]),
                        row(diff_lam_k2[0]), row(diff_subln[0]), batch, seq, lam_init)
    x2d = _out_mlp(x2d, o, diff_w_out[0].astype(BF16), row(norm_mlp[0]),
                   mlp_w1[0].astype(BF16), mlp_w2[0].astype(BF16))

    w_in = dsa_w_in[0]
    pad = jnp.zeros((d, LANES - IDX_DIM - IDX_HEADS), w_in.dtype)
    w_cat = jnp.concatenate([w_in, pad], axis=1).astype(BF16)
    q, qi, k, v, ki, wi = _proj_dsa(x2d, row(norm_mix[1]), w_cat, row(dsa_cq_norm[0]),
                                    dsa_w_uq[0].astype(BF16), dsa_w_uq_idx[0].astype(BF16),
                                    _tile_gain(dsa_q_norm[0], 256), _tile_gain(dsa_k_norm[0], 256),
                                    c, s1, s2)
    o = _dsa_attention(q, qi, wi, ki, k, v, batch, seq)
    x2d = _out_mlp(x2d, o, dsa_w_out[0].astype(BF16), row(norm_mlp[1]),
                   mlp_w1[1].astype(BF16), mlp_w2[1].astype(BF16))
    return x2d.reshape(batch, seq, d)
```

```python
import functools
import math

import numpy as np
import jax
import jax.numpy as jnp
from jax import lax
from jax.experimental import pallas as pl
from jax.experimental.pallas import tpu as pltpu

F32 = jnp.float32
BF16 = jnp.bfloat16

D_MODEL = 1024
CHUNK = 64
HEAD_DIM = 64
ROPE_THETA = 500000.0
ROPE_ROT = HEAD_DIM // 4
ROPE_HALF = ROPE_ROT // 2
NORM_EPS = 1e-6
DIFF_HEADS = D_MODEL // (2 * HEAD_DIM)
DSA_HEADS = D_MODEL // HEAD_DIM
DSA_Q_RANK = D_MODEL // 4
IDX_HEADS = 8
IDX_DIM = 64
TOPK = 256
D_FF = 4 * D_MODEL
LANES = 128
PAIRS = D_MODEL // LANES
NEG = -1e30
Q_SCALE = HEAD_DIM ** -0.5 * math.log2(math.e)
VMEM_LIMIT = 56 * 1024 * 1024

PROJ_ROWS = 512
PROJ_COLS = 512
MLP_ROWS = 512
DIFF_TILES = (512, 512)
DSA_TILES = (512, 512)
COUNT_ROWS = 128

INT_MIN, INT_MAX = -(2 ** 31), 2 ** 31 - 1
KEY_NEG_INF = INT_MIN + 0x7FFFFF


def _cparams(n_axes):
    return pltpu.CompilerParams(dimension_semantics=("arbitrary",) * n_axes,
                                vmem_limit_bytes=VMEM_LIMIT)


def _rms_rows(x):
    return x * lax.rsqrt(jnp.mean(x * x, axis=-1, keepdims=True) + NORM_EPS)


def _split_bf16(x):
    hi = x.astype(BF16)
    lo = (x - hi.astype(F32)).astype(BF16)
    return hi, lo


def _group_ones(n):
    r = lax.broadcasted_iota(jnp.int32, (n, n), 0) // HEAD_DIM
    c = lax.broadcasted_iota(jnp.int32, (n, n), 1) // HEAD_DIM
    return (r == c).astype(BF16)


def _head_rms(x, gmat):
    hi, lo = _split_bf16(x * x)
    ss = (jnp.dot(hi, gmat, preferred_element_type=F32)
          + jnp.dot(lo, gmat, preferred_element_type=F32))
    return x * lax.rsqrt(ss * (1.0 / HEAD_DIM) + NORM_EPS)


def _rope128(y, c, s1, s2):
    return y * c + pltpu.roll(y, ROPE_HALF, 1) * s1 + pltpu.roll(y, LANES - ROPE_HALF, 1) * s2


def _norm_rope_store(pc, gmat, gain, c, s1, s2, scale, out_ref, col0):
    y = pc if gmat is None else _head_rms(pc, gmat) * gain
    for hh in range(2):
        r = _rope128(y[:, hh * LANES:(hh + 1) * LANES], c, s1, s2)
        if scale != 1.0:
            r = r * scale
        out_ref[:, col0 + hh * LANES: col0 + (hh + 1) * LANES] = r.astype(out_ref.dtype)


def _proj_cols(hb, w_ref, col0, width):
    blocks = []
    for j in range(0, width, PROJ_COLS):
        w = min(PROJ_COLS, width - j)
        pc = jnp.dot(hb, w_ref[:, col0 + j: col0 + j + w], preferred_element_type=F32)
        blocks += [pc[:, i:i + 256] for i in range(0, w, 256)]
    return blocks


def _proj_diff_kernel(x_ref, g_ref, w_ref, qg_ref, kg_ref, c_ref, s1_ref, s2_ref,
                      q_out, k_out, v_out):
    hb = (_rms_rows(x_ref[...]) * g_ref[...]).astype(BF16)
    gmat = _group_ones(256)
    c, s1, s2 = c_ref[...], s1_ref[...], s2_ref[...]
    qg, kg = qg_ref[...], kg_ref[...]
    for j, pc in enumerate(_proj_cols(hb, w_ref, 0, D_MODEL)):
        _norm_rope_store(pc, gmat, qg, c, s1, s2, Q_SCALE, q_out, j * 256)
    for j, pc in enumerate(_proj_cols(hb, w_ref, D_MODEL, D_MODEL)):
        _norm_rope_store(pc, gmat, kg, c, s1, s2, 1.0, k_out, j * 256)
    for j, pc in enumerate(_proj_cols(hb, w_ref, 2 * D_MODEL, D_MODEL)):
        v_out[:, j * 256:(j + 1) * 256] = pc.astype(BF16)


def _proj_diff(x2d, g, w, qg, kg, c, s1, s2):
    n, tm = x2d.shape[0], PROJ_ROWS
    row = lambda i: (i, 0)
    fixed = lambda i: (0, 0)
    return pl.pallas_call(
        _proj_diff_kernel,
        grid=(n // tm,),
        in_specs=[
            pl.BlockSpec((tm, D_MODEL), row),
            pl.BlockSpec((1, D_MODEL), fixed),
            pl.BlockSpec((D_MODEL, 3 * D_MODEL), fixed),
            pl.BlockSpec((1, 256), fixed),
            pl.BlockSpec((1, 256), fixed),
            pl.BlockSpec((tm, LANES), row),
            pl.BlockSpec((tm, LANES), row),
            pl.BlockSpec((tm, LANES), row),
        ],
        out_specs=[pl.BlockSpec((tm, D_MODEL), row)] * 3,
        out_shape=[jax.ShapeDtypeStruct((n, D_MODEL), BF16)] * 3,
        compiler_params=_cparams(1),
        name="proj_diff",
    )(x2d, g, w, qg, kg, c, s1, s2)


def _stack_pair(q):
    lane = lax.broadcasted_iota(jnp.int32, q.shape, 1)
    zero = jnp.zeros_like(q)
    return jnp.concatenate([jnp.where(lane < HEAD_DIM, q, zero),
                            jnp.where(lane >= HEAD_DIM, q, zero)], axis=0)


def _qk(qq, k):
    return lax.dot_general(qq, k, (((1,), (1,)), ((), ())), preferred_element_type=F32)


def _flash_init(q_ref, qq_ref, m_ref, l_ref, acc_ref):
    for c in range(PAIRS):
        qq_ref[c] = _stack_pair(q_ref[:, c * LANES:(c + 1) * LANES])
    m_ref[...] = jnp.full(m_ref.shape, NEG, F32)
    l_ref[...] = jnp.zeros(l_ref.shape, F32)
    acc_ref[...] = jnp.zeros(acc_ref.shape, F32)


def _flash_tile(qq_ref, k_ref, v_ref, start, tk, bias, m_ref, l_ref, acc_ref):
    for c in range(PAIRS):
        cols = slice(c * LANES, (c + 1) * LANES)
        s = _qk(qq_ref[c], k_ref[pl.ds(start, tk), cols])
        if bias is not None:
            s = s + bias
        chunks = [s[:, u * LANES:(u + 1) * LANES] for u in range(tk // LANES)]
        m_prev = m_ref[c]
        m_cur = jnp.max(functools.reduce(jnp.maximum, chunks), axis=-1, keepdims=True)
        m_new = jnp.maximum(m_prev, m_cur)
        alpha = jnp.exp2(m_prev - m_new)
        ps = [jnp.exp2(ch - m_new) for ch in chunks]
        l_ref[c] = alpha * l_ref[c] + functools.reduce(jnp.add, ps)
        p = jnp.concatenate(ps, axis=1).astype(BF16)
        acc_ref[c] = alpha * acc_ref[c] + jnp.dot(p, v_ref[pl.ds(start, tk), cols],
                                                   preferred_element_type=F32)
        m_ref[c] = m_new


def _flash_out(c, l_ref, acc_ref):
    return acc_ref[c] / jnp.sum(l_ref[c], axis=-1, keepdims=True)


def _flash_scratch(tq):
    return [
        pltpu.VMEM((PAIRS, 2 * tq, LANES), BF16),
        pltpu.VMEM((PAIRS, 2 * tq, LANES), F32),
        pltpu.VMEM((PAIRS, 2 * tq, LANES), F32),
        pltpu.VMEM((PAIRS, 2 * tq, LANES), F32),
    ]


def _diff_attn_kernel(q_ref, k_ref, v_ref, lq1_ref, lk1_ref, lq2_ref, lk2_ref, sg_ref, o_ref,
                      qq_ref, m_ref, l_ref, acc_ref, *, tq, tk, lam_init):
    i = pl.program_id(1)
    _flash_init(q_ref, qq_ref, m_ref, l_ref, acc_ref)
    n_full = (i * tq) // tk

    def body(t, carry):
        _flash_tile(qq_ref, k_ref, v_ref, pl.multiple_of(t * tk, tk), tk, None, m_ref, l_ref, acc_ref)
        return carry

    lax.fori_loop(0, n_full, body, 0)

    start = pl.multiple_of(n_full * tk, tk)
    qc = (i * tq + lax.broadcasted_iota(jnp.int32, (2 * tq, tk), 0) % tq) // CHUNK
    kc = (start + lax.broadcasted_iota(jnp.int32, (2 * tq, tk), 1)) // CHUNK
    bias = jnp.where(kc <= qc, 0.0, NEG)
    _flash_tile(qq_ref, k_ref, v_ref, start, tk, bias, m_ref, l_ref, acc_ref)

    lam = (jnp.exp(jnp.sum(lq1_ref[...] * lk1_ref[...], axis=-1, keepdims=True))
           - jnp.exp(jnp.sum(lq2_ref[...] * lk2_ref[...], axis=-1, keepdims=True)) + lam_init)
    for c in range(PAIRS):
        o = _flash_out(c, l_ref, acc_ref)
        o = o[:tq] - lam * o[tq:]
        o = _rms_rows(o) * sg_ref[...] * (1.0 - lam_init)
        o_ref[:, c * LANES:(c + 1) * LANES] = o.astype(o_ref.dtype)


def _diff_attention(q, k, v, lq1, lk1, lq2, lk2, sg, batch, seq, lam_init):
    tq, tk = DIFF_TILES
    nq = seq // tq
    qmap = lambda b, i: (b * nq + i, 0)
    kvmap = lambda b, i: (b, 0)
    fixed = lambda b, i: (0, 0)
    once = pl.Buffered(1)
    kern = functools.partial(_diff_attn_kernel, tq=tq, tk=tk, lam_init=lam_init)
    return pl.pallas_call(
        kern,
        grid=(batch, nq),
        in_specs=[
            pl.BlockSpec((tq, D_MODEL), qmap),
            pl.BlockSpec((seq, D_MODEL), kvmap, pipeline_mode=once),
            pl.BlockSpec((seq, D_MODEL), kvmap, pipeline_mode=once),
            pl.BlockSpec((1, HEAD_DIM), fixed),
            pl.BlockSpec((1, HEAD_DIM), fixed),
            pl.BlockSpec((1, HEAD_DIM), fixed),
            pl.BlockSpec((1, HEAD_DIM), fixed),
            pl.BlockSpec((1, LANES), fixed),
        ],
        out_specs=pl.BlockSpec((tq, D_MODEL), qmap),
        out_shape=jax.ShapeDtypeStruct((batch * seq, D_MODEL), BF16),
        scratch_shapes=_flash_scratch(tq),
        compiler_params=_cparams(2),
        name="diff_attention",
    )(q, k, v, lq1, lk1, lq2, lk2, sg)


def _out_mlp_kernel(x_ref, o_ref, wo_ref, g_ref, w1_ref, w2_ref, out_ref):
    x1 = x_ref[...] + jnp.dot(o_ref[...], wo_ref[...], preferred_element_type=F32)
    hn = (_rms_rows(x1) * g_ref[...]).astype(BF16)
    acc = x1
    for j in range(D_FF // D_MODEL):
        u = jnp.dot(hn, w1_ref[:, j * D_MODEL:(j + 1) * D_MODEL], preferred_element_type=F32)
        u = jnp.maximum(u, 0.0)
        acc = acc + jnp.dot((u * u).astype(BF16), w2_ref[j * D_MODEL:(j + 1) * D_MODEL, :],
                            preferred_element_type=F32)
    out_ref[...] = acc


def _out_mlp(x2d, o, wo, g, w1, w2):
    n, tm = x2d.shape[0], MLP_ROWS
    row = lambda i: (i, 0)
    fixed = lambda i: (0, 0)
    once = pl.Buffered(1)
    return pl.pallas_call(
        _out_mlp_kernel,
        grid=(n // tm,),
        in_specs=[
            pl.BlockSpec((tm, D_MODEL), row),
            pl.BlockSpec((tm, D_MODEL), row),
            pl.BlockSpec((D_MODEL, D_MODEL), fixed, pipeline_mode=once),
            pl.BlockSpec((1, D_MODEL), fixed),
            pl.BlockSpec((D_MODEL, D_FF), fixed, pipeline_mode=once),
            pl.BlockSpec((D_FF, D_MODEL), fixed, pipeline_mode=once),
        ],
        out_specs=pl.BlockSpec((tm, D_MODEL), row),
        out_shape=jax.ShapeDtypeStruct((n, D_MODEL), F32),
        compiler_params=_cparams(1),
        name="out_mlp",
    )(x2d, o, wo, g, w1, w2)


DSA_TAIL = D_MODEL // 4 + 2 * D_MODEL


def _proj_dsa_kernel(x_ref, g_ref, w_ref, cqg_ref, wuq_ref, wui_ref, qg_ref, kg_ref,
                     c_ref, s1_ref, s2_ref, q_out, qi_out, k_out, v_out, ki_out, wi_out):
    hb = (_rms_rows(x_ref[...]) * g_ref[...]).astype(BF16)
    gmat = _group_ones(256)
    c, s1, s2 = c_ref[...], s1_ref[...], s2_ref[...]

    cq = jnp.dot(hb, w_ref[:, 0:DSA_Q_RANK], preferred_element_type=F32)
    cqb = (_rms_rows(cq) * cqg_ref[...]).astype(BF16)
    qg, kg = qg_ref[...], kg_ref[...]
    for j, pc in enumerate(_proj_cols(cqb, wuq_ref, 0, D_MODEL)):
        _norm_rope_store(pc, gmat, qg, c, s1, s2, Q_SCALE, q_out, j * 256)
    for j, pc in enumerate(_proj_cols(cqb, wui_ref, 0, IDX_HEADS * IDX_DIM)):
        _norm_rope_store(pc, None, None, c, s1, s2, 1.0, qi_out, j * 256)
    for j, pc in enumerate(_proj_cols(hb, w_ref, DSA_Q_RANK, D_MODEL)):
        _norm_rope_store(pc, gmat, kg, c, s1, s2, 1.0, k_out, j * 256)
    for j, pc in enumerate(_proj_cols(hb, w_ref, DSA_Q_RANK + D_MODEL, D_MODEL)):
        v_out[:, j * 256:(j + 1) * 256] = pc.astype(BF16)

    t = jnp.dot(hb, w_ref[:, DSA_TAIL:DSA_TAIL + LANES], preferred_element_type=F32)
    lane = lax.broadcasted_iota(jnp.int32, t.shape, 1)
    is_k = lane < IDX_DIM
    ss = jnp.sum(jnp.where(is_k, t * t, 0.0), axis=-1, keepdims=True)
    kn = t * lax.rsqrt(ss * (1.0 / IDX_DIM) + NORM_EPS)
    kr = _rope128(kn, c, s1, s2)
    ki_out[...] = jnp.where(is_k, kr, pltpu.roll(kr, IDX_DIM, 1)).astype(BF16)
    wi_out[...] = t * ((IDX_HEADS ** -0.5) * (IDX_DIM ** -0.5))


def _proj_dsa(x2d, g, w, cqg, wuq, wui, qg, kg, c, s1, s2):
    n, tm = x2d.shape[0], PROJ_ROWS
    row = lambda i: (i, 0)
    fixed = lambda i: (0, 0)
    wcols = w.shape[1]
    idx_w = IDX_HEADS * IDX_DIM
    return pl.pallas_call(
        _proj_dsa_kernel,
        grid=(n // tm,),
        in_specs=[
            pl.BlockSpec((tm, D_MODEL), row),
            pl.BlockSpec((1, D_MODEL), fixed),
            pl.BlockSpec((D_MODEL, wcols), fixed),
            pl.BlockSpec((1, DSA_Q_RANK), fixed),
            pl.BlockSpec((DSA_Q_RANK, D_MODEL), fixed),
            pl.BlockSpec((DSA_Q_RANK, idx_w), fixed),
            pl.BlockSpec((1, 256), fixed),
            pl.BlockSpec((1, 256), fixed),
            pl.BlockSpec((tm, LANES), row),
            pl.BlockSpec((tm, LANES), row),
            pl.BlockSpec((tm, LANES), row),
        ],
        out_specs=[
            pl.BlockSpec((tm, D_MODEL), row),
            pl.BlockSpec((tm, idx_w), row),
            pl.BlockSpec((tm, D_MODEL), row),
            pl.BlockSpec((tm, D_MODEL), row),
            pl.BlockSpec((tm, LANES), row),
            pl.BlockSpec((tm, LANES), row),
        ],
        out_shape=[
            jax.ShapeDtypeStruct((n, D_MODEL), BF16),
            jax.ShapeDtypeStruct((n, idx_w), BF16),
            jax.ShapeDtypeStruct((n, D_MODEL), BF16),
            jax.ShapeDtypeStruct((n, D_MODEL), BF16),
            jax.ShapeDtypeStruct((n, LANES), BF16),
            jax.ShapeDtypeStruct((n, LANES), F32),
        ],
        compiler_params=_cparams(1),
        name="proj_dsa",
    )(x2d, g, w, cqg, wuq, wui, qg, kg, c, s1, s2)


def _sortable_key(x):
    bits = lax.bitcast_convert_type(x, jnp.int32)
    return bits ^ (lax.shift_right_arithmetic(bits, 31) & 0x7FFFFFFF)


def _visible_tiles(i, tq, tk):
    return (i * tq + tq + tk - 1) // tk


def _indexer_keys(qi_ref, wi_ref, ki_ref, key_ref, qis_ref, wrep_ref, row0, ntiles, tq, tk):
    wi = wi_ref[...]
    for c in range(IDX_HEADS // 2):
        qis_ref[c] = _stack_pair(qi_ref[:, c * LANES:(c + 1) * LANES])
        w0 = wi[:, IDX_DIM + 2 * c: IDX_DIM + 2 * c + 1]
        w1 = wi[:, IDX_DIM + 2 * c + 1: IDX_DIM + 2 * c + 2]
        wrep_ref[c] = jnp.broadcast_to(jnp.concatenate([w0, w1], axis=0), (2 * tq, LANES))
    qc = (row0 + lax.broadcasted_iota(jnp.int32, (tq, LANES), 0)) // CHUNK
    lane = lax.broadcasted_iota(jnp.int32, (tq, LANES), 1)

    def body(t, carry):
        start = pl.multiple_of(t * tk, tk)
        kd = ki_ref[pl.ds(start, tk), :]
        score = [None] * (tk // LANES)
        for c in range(IDX_HEADS // 2):
            d = _qk(qis_ref[c], kd)
            w = wrep_ref[c]
            for u in range(tk // LANES):
                r = jnp.maximum(d[:, u * LANES:(u + 1) * LANES], 0.0) * w
                r = r[:tq] + r[tq:]
                score[u] = r if score[u] is None else score[u] + r
        keys = []
        for u in range(tk // LANES):
            kc = (start + u * LANES + lane) // CHUNK
            keys.append(jnp.where(kc <= qc, _sortable_key(score[u]), KEY_NEG_INF))
        key_ref[:, pl.ds(start, tk)] = jnp.concatenate(keys, axis=1)
        return carry

    lax.fori_loop(0, ntiles, body, 0)


def _count_rows(key_ref, ntiles, tq, tk, pred, *row_vals):
    parts = []
    for r in range(tq // COUNT_ROWS):
        rows = slice(r * COUNT_ROWS, (r + 1) * COUNT_ROWS)
        vals = [a[rows] for a in row_vals]

        def body(t, cnt, rows=rows, vals=vals):
            start = pl.multiple_of(t * tk, tk)
            kt = key_ref[rows, pl.ds(start, tk)]
            for u in range(tk // LANES):
                hit = pred(kt[:, u * LANES:(u + 1) * LANES], start + u * LANES, *vals)
                cnt = cnt + hit.astype(jnp.int32)
            return cnt

        parts.append(lax.fori_loop(0, ntiles, body, jnp.zeros((COUNT_ROWS, LANES), jnp.int32)))
    total = jnp.sum(jnp.concatenate(parts, axis=0), axis=-1, keepdims=True)
    return jnp.broadcast_to(total, (tq, LANES))


def _bit_search(key_ref, ntiles, tq, tk, thr0, cnt0, nbits):
    def bit_body(b, carry):
        thr, cnt_thr = carry
        cand = thr + lax.shift_left(jnp.int32(1), nbits - 1 - b)
        total = _count_rows(key_ref, ntiles, tq, tk, lambda kt, col, c: kt >= c, cand)
        keep = total >= TOPK
        return jnp.where(keep, cand, thr), jnp.where(keep, total, cnt_thr)

    return lax.fori_loop(0, nbits, bit_body, (thr0, cnt0))


def _topk_threshold(key_ref, ntiles, tq, tk):
    thr0 = jnp.full((tq, LANES), INT_MIN, jnp.int32)
    cnt0 = jnp.zeros((tq, LANES), jnp.int32) + ntiles * tk
    thr, cnt_thr = _bit_search(key_ref, ntiles, tq, tk, thr0, cnt0, 32)
    few = thr <= KEY_NEG_INF
    return jnp.where(few, KEY_NEG_INF + 1, thr), jnp.where(few, 0, cnt_thr - TOPK)


def _rank_ties(key_ref, thr, ntiles, tq, tk, seq):
    lane = lax.broadcasted_iota(jnp.int32, (tq, LANES), 1)

    def rewrite(t, carry):
        start = pl.multiple_of(t * tk, tk)
        kt = key_ref[:, pl.ds(start, tk)]
        out = []
        for u in range(tk // LANES):
            ku = kt[:, u * LANES:(u + 1) * LANES]
            rank = (seq - 1 - u * LANES - start) - lane
            out.append(jnp.where(ku > thr, INT_MAX, jnp.where(ku == thr, rank, INT_MIN)))
        key_ref[:, pl.ds(start, tk)] = jnp.concatenate(out, axis=1)
        return carry

    lax.fori_loop(0, ntiles, rewrite, 0)
    zero = jnp.zeros((tq, LANES), jnp.int32)
    return _bit_search(key_ref, ntiles, tq, tk, zero, zero, (seq - 1).bit_length())[0]


def _dsa_kernel(qblk_ref, tile_ref, last_ref, q_ref, qi_ref, wi_ref, ki_ref, k_ref, v_ref, o_ref,
                key_ref, thr_ref, qis_ref, wrep_ref, qq_ref, m_ref, l_ref, acc_ref, *, tq, tk, seq):
    step = pl.program_id(1)
    i = qblk_ref[step]
    t = tile_ref[step]

    @pl.when(t == 0)
    def _():
        ntiles = _visible_tiles(i, tq, tk)
        _indexer_keys(qi_ref, wi_ref, ki_ref, key_ref, qis_ref, wrep_ref, i * tq, ntiles, tq, tk)
        thr, surplus = _topk_threshold(key_ref, ntiles, tq, tk)

        thr_ref[...] = thr

        @pl.when(jnp.max(surplus.astype(F32)) > 0.0)
        def _():
            thr_ref[...] = _rank_ties(key_ref, thr, ntiles, tq, tk, seq)

        _flash_init(q_ref, qq_ref, m_ref, l_ref, acc_ref)

    start = pl.multiple_of(t * tk, tk)
    thr = thr_ref[...]
    kt = key_ref[:, pl.ds(start, tk)]
    bias = jnp.concatenate([jnp.where(kt[:, u * LANES:(u + 1) * LANES] >= thr, 0.0, NEG)
                            for u in range(tk // LANES)], axis=1)
    bias = jnp.concatenate([bias, bias], axis=0)
    _flash_tile(qq_ref, k_ref, v_ref, 0, tk, bias, m_ref, l_ref, acc_ref)

    @pl.when(last_ref[step] == 1)
    def _():
        lane = lax.broadcasted_iota(jnp.int32, (tq, LANES), 1)
        for c in range(PAIRS):
            o = _flash_out(c, l_ref, acc_ref)
            o_ref[:, c * LANES:(c + 1) * LANES] = jnp.where(lane < HEAD_DIM, o[:tq], o[tq:]).astype(o_ref.dtype)


def _dsa_schedule(seq, tq, tk):
    qblk, tile, last = [], [], []
    for i in range(seq // tq):
        nt = _visible_tiles(i, tq, tk)
        for t in range(nt):
            qblk.append(i)
            tile.append(t)
            last.append(int(t == nt - 1))
    return tuple(jnp.asarray(np.asarray(a, np.int32)) for a in (qblk, tile, last))


def _dsa_attention(q, qi, wi, ki, k, v, batch, seq):
    tq, tk = DSA_TILES
    nq, nk = seq // tq, seq // tk
    qblk, tile, last = _dsa_schedule(seq, tq, tk)
    qmap = lambda b, s, qb, tl, ls: (b * nq + qb[s], 0)
    kvmap = lambda b, s, qb, tl, ls: (b * nk + tl[s], 0)
    bmap = lambda b, s, qb, tl, ls: (b, 0)
    kern = functools.partial(_dsa_kernel, tq=tq, tk=tk, seq=seq)
    idx_pairs = IDX_HEADS // 2
    grid_spec = pltpu.PrefetchScalarGridSpec(
        num_scalar_prefetch=3,
        grid=(batch, int(qblk.shape[0])),
        in_specs=[
            pl.BlockSpec((tq, D_MODEL), qmap),
            pl.BlockSpec((tq, IDX_HEADS * IDX_DIM), qmap),
            pl.BlockSpec((tq, LANES), qmap),
            pl.BlockSpec((seq, LANES), bmap),
            pl.BlockSpec((tk, D_MODEL), kvmap),
            pl.BlockSpec((tk, D_MODEL), kvmap),
        ],
        out_specs=pl.BlockSpec((tq, D_MODEL), qmap),
        scratch_shapes=[
            pltpu.VMEM((tq, seq), jnp.int32),
            pltpu.VMEM((tq, LANES), jnp.int32),
            pltpu.VMEM((idx_pairs, 2 * tq, LANES), BF16),
            pltpu.VMEM((idx_pairs, 2 * tq, LANES), F32),
        ] + _flash_scratch(tq),
    )
    return pl.pallas_call(
        kern,
        grid_spec=grid_spec,
        out_shape=jax.ShapeDtypeStruct((batch * seq, D_MODEL), BF16),
        compiler_params=_cparams(2),
        name="dsa_attention",
    )(qblk, tile, last, q, qi, wi, ki, k, v)


def _rope_lane_tables(positions):
    n = positions.size
    inv = ROPE_THETA ** (-jnp.arange(0, ROPE_ROT, 2, dtype=F32) / ROPE_ROT)
    ang = positions.reshape(n, 1).astype(F32) * inv
    cos, sin = jnp.cos(ang), jnp.sin(ang)
    rest = HEAD_DIM - ROPE_ROT
    c = jnp.concatenate([cos, cos, jnp.ones((n, rest), F32)], axis=1)
    s1 = jnp.concatenate([jnp.zeros((n, ROPE_HALF), F32), sin, jnp.zeros((n, rest), F32)], axis=1)
    s2 = jnp.concatenate([-sin, jnp.zeros((n, HEAD_DIM - ROPE_HALF), F32)], axis=1)
    return tuple(jnp.tile(t, (1, LANES // HEAD_DIM)) for t in (c, s1, s2))


def _tile_gain(g, width):
    return jnp.tile(g.astype(F32), width // g.shape[0]).reshape(1, width)


def kernel(x, positions, norm_mix, norm_mlp, mlp_w1, mlp_w2, diff_w_in, diff_q_norm, diff_k_norm, diff_lam_q1, diff_lam_k1, diff_lam_q2, diff_lam_k2, diff_subln, diff_w_out, dsa_w_in, dsa_cq_norm, dsa_w_uq, dsa_w_uq_idx, dsa_q_norm, dsa_k_norm, dsa_w_out):
    batch, seq, d = x.shape
    n = batch * seq
    x2d = x.reshape(n, d)
    c, s1, s2 = _rope_lane_tables(positions)
    row = lambda a: a.astype(F32).reshape(1, -1)

    lam_init = 0.8 - 0.6 * math.exp(-0.3 * 0)
    q, k, v = _proj_diff(x2d, row(norm_mix[0]), diff_w_in[0].astype(BF16),
                         _tile_gain(diff_q_norm[0], 256), _tile_gain(diff_k_norm[0], 256), c, s1, s2)
    o = _diff_attention(q, k, v, row(diff_lam_q1[0]), row(diff_lam_k1[0]), row(diff_lam_q2[0]),
                        row(diff_lam_k2[0]), row(diff_subln[0]), batch, seq, lam_init)
    x2d = _out_mlp(x2d, o, diff_w_out[0].astype(BF16), row(norm_mlp[0]),
                   mlp_w1[0].astype(BF16), mlp_w2[0].astype(BF16))

    w_in = dsa_w_in[0]
    pad = jnp.zeros((d, LANES - IDX_DIM - IDX_HEADS), w_in.dtype)
    w_cat = jnp.concatenate([w_in, pad], axis=1).astype(BF16)
    q, qi, k, v, ki, wi = _proj_dsa(x2d, row(norm_mix[1]), w_cat, row(dsa_cq_norm[0]),
                                    dsa_w_uq[0].astype(BF16), dsa_w_uq_idx[0].astype(BF16),
                                    _tile_gain(dsa_q_norm[0], 256), _tile_gain(dsa_k_norm[0], 256),
                                    c, s1, s2)
    o = _dsa_attention(q, qi, wi, ki, k, v, batch, seq)
    x2d = _out_mlp(x2d, o, dsa_w_out[0].astype(BF16), row(norm_mlp[1]),
                   mlp_w1[1].astype(BF16), mlp_w2[1].astype(BF16))
    return x2d.reshape(batch, seq, d)
```
